```python
import math
import jax, jax.numpy as jnp
from jax import lax
import numpy as np

D_MODEL = 2048
BATCH = 2
SEQ = 4096
DEPTH = 2
DEC_BATCH = 128
DEC_SEQ = 8
PAST_LEN = 8192
PAGE_SIZE = 128

F32 = jnp.float32
N_EVEN_LAYERS = (DEPTH + 1) // 2
N_ODD_LAYERS = DEPTH // 2
RET_HEADS = 8
RET_DK = 128
RET_DV = 128
RET_CHUNK = 128
MLA_HEADS = 8
MLA_NOPE = 128
MLA_ROPE = 64
MLA_V = 128
MLA_Q_RANK = 512
MLA_KV_RANK = 512
MLA_SCALE = (MLA_NOPE + MLA_ROPE) ** -0.5
SB_HEADS = 16
SB_KV_HEADS = 4
SB_GROUP = SB_HEADS // SB_KV_HEADS
SB_DIM = 128
SB_SCALE = SB_DIM ** -0.5
Q_BLOCK = 128
D_FF = 5632
ROPE_BASE = 10000.0
EPS = 1e-5
ALPHA = (2 * DEPTH) ** 0.25
BETA_INIT = (8 * DEPTH) ** -0.25
EVEN_WIDTHS = (RET_HEADS * RET_DK, RET_HEADS * RET_DK, RET_HEADS * RET_DV, RET_HEADS * RET_DV, MLA_Q_RANK, MLA_KV_RANK, MLA_ROPE)
EVEN_IN = sum(EVEN_WIDTHS)
EVEN_SPLITS = tuple(int(c) for c in np.cumsum(EVEN_WIDTHS)[:-1])
EVEN_OUT = RET_HEADS * RET_DV + MLA_HEADS * MLA_V
SB_WIDTHS = (SB_HEADS * SB_DIM, SB_KV_HEADS * SB_DIM, SB_KV_HEADS * SB_DIM)
SB_IN = sum(SB_WIDTHS)
SB_SPLITS = (SB_WIDTHS[0], SB_WIDTHS[0] + SB_WIDTHS[1])

kernel_name = 'hybrid_retention_mla_stickbreaking_decode_step'


def layer_norm(x, g, b):
    xf = x.astype(F32)
    mu = jnp.mean(xf, -1, keepdims=True)
    var = jnp.mean(jnp.square(xf - mu), -1, keepdims=True)
    return ((xf - mu) * lax.rsqrt(var + EPS) * g + b).astype(x.dtype)


def rms_norm(x, g):
    xf = x.astype(F32)
    return (xf * lax.rsqrt(jnp.mean(xf * xf, -1, keepdims=True) + EPS) * g).astype(x.dtype)


def ffn_half_step(x, w_gate, w_up, w_down, g, b):
    h = (jax.nn.silu(x @ w_gate) * (x @ w_up)) @ w_down
    return layer_norm(ALPHA * x + 0.5 * h, g, b)


def ret_inv_freq():
    return 1.0 / (ROPE_BASE ** jnp.linspace(0.0, 1.0, RET_DK // 2, dtype=F32))


def mla_inv_freq():
    return 1.0 / (ROPE_BASE ** (jnp.arange(0, MLA_ROPE, 2, dtype=F32) / MLA_ROPE))


def angles(pos, inv_freq):
    return pos.astype(F32)[:, None] * inv_freq[None, :]


def rotate(x, ang):
    c = jnp.cos(ang).astype(x.dtype)
    s = jnp.sin(ang).astype(x.dtype)
    x1, x2 = jnp.split(x, 2, axis=-1)
    return jnp.concatenate([x1 * c - x2 * s, x1 * s + x2 * c], axis=-1)


def retention_log_decay():
    return jnp.log(1.0 - jnp.exp2(-5.0 - jnp.arange(RET_HEADS, dtype=F32)))


def even_inputs(x, pos, w_in, q_norm_g, w_uq, kv_norm_g):
    B, T, _ = x.shape
    q_a, k_a, v_a, g_a, c_q, c_kv, k_pe = jnp.split(x @ w_in, EVEN_SPLITS, axis=-1)
    ang_r = angles(pos, ret_inv_freq())[:, None, :]
    q_r = rotate(q_a.reshape(B, T, RET_HEADS, RET_DK), ang_r)
    k_r = rotate(k_a.reshape(B, T, RET_HEADS, RET_DK), ang_r) * (RET_DK ** -0.5)
    v_r = v_a.reshape(B, T, RET_HEADS, RET_DV)
    q_m = (rms_norm(c_q, q_norm_g) @ w_uq).reshape(B, T, MLA_HEADS, MLA_NOPE + MLA_ROPE)
    ang_m = angles(pos, mla_inv_freq())
    q_nope = q_m[..., :MLA_NOPE]
    q_pe = rotate(q_m[..., MLA_NOPE:], ang_m[:, None, :])
    c_kv = rms_norm(c_kv, kv_norm_g)
    k_pe = rotate(k_pe, ang_m)
    return q_r, k_r, v_r, g_a, q_nope, q_pe, c_kv, k_pe


def retention_chunk(S0, q, k, v, log_gamma):
    L = q.shape[1]
    idx = jnp.arange(L, dtype=F32)
    diff = idx[:, None] - idx[None, :]
    decay = jnp.where(diff >= 0, jnp.exp(jnp.maximum(diff, 0.0)[None] * log_gamma[:, None, None]), 0.0).astype(q.dtype)
    s = jnp.einsum('bihd,bjhd->bhij', q, k) * decay
    o = jnp.einsum('bhij,bjhv->bihv', s, v)
    q_decay = jnp.exp((idx + 1.0)[:, None] * log_gamma[None, :]).astype(q.dtype)
    o = o + jnp.einsum('bihd,bhdv->bihv', q, S0) * q_decay[None, :, :, None]
    k_decay = jnp.exp((L - 1.0 - idx)[:, None] * log_gamma[None, :]).astype(q.dtype)
    chunk_decay = jnp.exp(L * log_gamma).astype(S0.dtype)
    S_new = S0 * chunk_decay[None, :, None, None] + jnp.einsum('bjhd,bjhv->bhdv', k * k_decay[None, :, :, None], v)
    return o, S_new.astype(S0.dtype)


def retention_prompt(q, k, v, log_gamma):
    B, T, H, dk = q.shape
    nc = T // RET_CHUNK

    def to_chunks(a):
        return a.reshape(B, nc, RET_CHUNK, H, a.shape[-1]).swapaxes(0, 1)

    def step(S, qkv):
        o, S = retention_chunk(S, qkv[0], qkv[1], qkv[2], log_gamma)
        return S, o

    S0 = jnp.zeros((B, H, dk, v.shape[-1]), q.dtype)
    S, o = lax.scan(step, S0, (to_chunks(q), to_chunks(k), to_chunks(v)))
    return o.swapaxes(0, 1).reshape(B, T, H, v.shape[-1]), S


def retention_out(o, g, gn_g, gn_b):
    B, T = o.shape[:2]
    of = o.astype(F32)
    mu = jnp.mean(of, -1, keepdims=True)
    var = jnp.mean(jnp.square(of - mu), -1, keepdims=True)
    on = ((of - mu) * lax.rsqrt(var + EPS) * gn_g + gn_b).astype(o.dtype)
    return on.reshape(B, T, RET_HEADS * RET_DV) * jax.nn.silu(g)


def mla_prompt(q_nope, q_pe, c_kv, k_pe, w_ukv):
    B, T = q_nope.shape[:2]
    kv = (c_kv @ w_ukv).reshape(B, T, MLA_HEADS, MLA_NOPE + MLA_V)
    k_nope, v = kv[..., :MLA_NOPE], kv[..., MLA_NOPE:]
    kpos = jnp.arange(T)

    def block(i):
        qs = i * Q_BLOCK
        qn = lax.dynamic_slice_in_dim(q_nope, qs, Q_BLOCK, axis=1)
        qp = lax.dynamic_slice_in_dim(q_pe, qs, Q_BLOCK, axis=1)
        s = (jnp.einsum('bqhd,bkhd->bhqk', qn, k_nope) + jnp.einsum('bqhd,bkd->bhqk', qp, k_pe)).astype(F32) * MLA_SCALE
        qpos = qs + jnp.arange(Q_BLOCK)
        s = jnp.where(kpos[None, :] <= qpos[:, None], s, -jnp.inf)
        p = jax.nn.softmax(s, axis=-1).astype(v.dtype)
        return jnp.einsum('bhqk,bkhd->bqhd', p, v)

    o = lax.map(block, jnp.arange(T // Q_BLOCK))
    return o.transpose(1, 0, 2, 3, 4).reshape(B, T, MLA_HEADS * MLA_V)


def mla_sample(q_nope, q_pe, c_kv, k_pe, w_ukv, cache_ckv, cache_kpe, layer, page_table):
    Bd, T = q_nope.shape[:2]
    w = w_ukv.reshape(MLA_KV_RANK, MLA_HEADS, MLA_NOPE + MLA_V)
    w_uk, w_uv = w[..., :MLA_NOPE], w[..., MLA_NOPE:]
    q_lat = jnp.einsum('bqhd,rhd->bqhr', q_nope, w_uk)

    def one(args):
        pages, ql, qp, ckv_new, kpe_new = args
        ckv = jnp.concatenate([cache_ckv[layer, pages].reshape(-1, MLA_KV_RANK), ckv_new], axis=0)
        kpe = jnp.concatenate([cache_kpe[layer, pages].reshape(-1, MLA_ROPE), kpe_new], axis=0)
        past = pages.shape[0] * PAGE_SIZE
        s = (jnp.einsum('qhr,kr->hqk', ql, ckv) + jnp.einsum('qhd,kd->hqk', qp, kpe)).astype(F32) * MLA_SCALE
        kpos = jnp.arange(past + T)
        qpos = past + jnp.arange(T)
        s = jnp.where(kpos[None, None, :] <= qpos[None, :, None], s, -jnp.inf)
        p = jax.nn.softmax(s, axis=-1).astype(ckv.dtype)
        return jnp.einsum('hqk,kr->qhr', p, ckv)

    o_lat = lax.map(one, (page_table, q_lat, q_pe, c_kv, k_pe))
    return jnp.einsum('bqhr,rhv->bqhv', o_lat, w_uv).reshape(Bd, T, MLA_HEADS * MLA_V)


def stick_breaking(q, k, v, qpos, kpos):
    z = jnp.einsum('...hgqd,...hkd->...hgqk', q, k).astype(F32) * SB_SCALE
    mask = kpos[None, :] < qpos[:, None]
    log_1mb = jnp.where(mask, jax.nn.log_sigmoid(-z), 0.0)
    surv = lax.cumsum(log_1mb, axis=z.ndim - 1, reverse=True) - log_1mb
    a = jnp.where(mask, jnp.exp(jax.nn.log_sigmoid(z) + surv), 0.0)
    return jnp.einsum('...hgqk,...hkd->...hgqd', a.astype(v.dtype), v)


def sb_project(x, w_in):
    B, T, _ = x.shape
    q, k, v = jnp.split(x @ w_in, SB_SPLITS, axis=-1)
    return (q.reshape(B, T, SB_KV_HEADS, SB_GROUP, SB_DIM),
            k.reshape(B, T, SB_KV_HEADS, SB_DIM),
            v.reshape(B, T, SB_KV_HEADS, SB_DIM))


def sb_prompt(q, k, v):
    B, T = q.shape[:2]
    qh = q.transpose(0, 2, 3, 1, 4)
    kh = k.transpose(0, 2, 1, 3)
    vh = v.transpose(0, 2, 1, 3)
    kpos = jnp.arange(T)

    def block(i):
        qs = i * Q_BLOCK
        qb = lax.dynamic_slice_in_dim(qh, qs, Q_BLOCK, axis=3)
        return stick_breaking(qb, kh, vh, qs + jnp.arange(Q_BLOCK), kpos)

    o = lax.map(block, jnp.arange(T // Q_BLOCK))
    return o.transpose(1, 0, 4, 2, 3, 5).reshape(B, T, SB_HEADS * SB_DIM)


def sb_sample(q, k, v, cache_k, cache_v, layer, page_table):
    T = q.shape[1]

    def one(args):
        pages, qi, ki, vi = args
        kk = jnp.concatenate([cache_k[layer, pages].reshape(-1, SB_KV_HEADS, SB_DIM), ki], axis=0)
        vv = jnp.concatenate([cache_v[layer, pages].reshape(-1, SB_KV_HEADS, SB_DIM), vi], axis=0)
        past = pages.shape[0] * PAGE_SIZE
        o = stick_breaking(qi.transpose(1, 2, 0, 3), kk.transpose(1, 0, 2), vv.transpose(1, 0, 2),
                           past + jnp.arange(T), jnp.arange(past + T))
        return o.transpose(2, 0, 1, 3).reshape(T, SB_HEADS * SB_DIM)

    return lax.map(one, (page_table, q, k, v))


def setup_inputs(seed: int = 0) -> dict:
    key = jax.random.key(seed)
    keys = jax.random.split(key, 32)
    ks = iter(list(keys))

    def nrm(shape, scale):
        return jax.random.normal(next(ks), shape, F32) * scale

    n_pages = PAST_LEN // PAGE_SIZE
    n_phys = (5 * DEC_BATCH * n_pages) // 4
    page_table = jax.random.permutation(next(ks), n_phys)[: DEC_BATCH * n_pages].reshape(DEC_BATCH, n_pages).astype(jnp.int32)
    return {
        'x_prompt': nrm((BATCH, SEQ, D_MODEL), 1.0),
        'x_sample': nrm((DEC_BATCH, DEC_SEQ, D_MODEL), 1.0),
        'state_ret': nrm((N_EVEN_LAYERS, DEC_BATCH, RET_HEADS, RET_DK, RET_DV), 0.5),
        'cache_mla_ckv': nrm((N_EVEN_LAYERS, n_phys, PAGE_SIZE, MLA_KV_RANK), 1.0),
        'cache_mla_kpe': nrm((N_EVEN_LAYERS, n_phys, PAGE_SIZE, MLA_ROPE), 1.0),
        'cache_sb_k': nrm((N_ODD_LAYERS, n_phys, PAGE_SIZE, SB_KV_HEADS, SB_DIM), 1.0),
        'cache_sb_v': nrm((N_ODD_LAYERS, n_phys, PAGE_SIZE, SB_KV_HEADS, SB_DIM), 1.0),
        'page_table': page_table,
        'ln_g': 1.0 + nrm((DEPTH, 3, D_MODEL), 0.02),
        'ln_b': nrm((DEPTH, 3, D_MODEL), 0.02),
        'ffn_w_gate': nrm((DEPTH, 2, D_MODEL, D_FF), D_MODEL ** -0.5),
        'ffn_w_up': nrm((DEPTH, 2, D_MODEL, D_FF), D_MODEL ** -0.5),
        'ffn_w_down': nrm((DEPTH, 2, D_FF, D_MODEL), BETA_INIT * D_FF ** -0.5),
        'ev_w_in': nrm((N_EVEN_LAYERS, D_MODEL, EVEN_IN), D_MODEL ** -0.5),
        'ev_q_norm': 1.0 + nrm((N_EVEN_LAYERS, MLA_Q_RANK), 0.02),
        'ev_w_uq': nrm((N_EVEN_LAYERS, MLA_Q_RANK, MLA_HEADS * (MLA_NOPE + MLA_ROPE)), MLA_Q_RANK ** -0.5),
        'ev_kv_norm': 1.0 + nrm((N_EVEN_LAYERS, MLA_KV_RANK), 0.02),
        'ev_w_ukv': nrm((N_EVEN_LAYERS, MLA_KV_RANK, MLA_HEADS * (MLA_NOPE + MLA_V)), MLA_KV_RANK ** -0.5),
        'ev_gn_g': 1.0 + nrm((N_EVEN_LAYERS, RET_HEADS, RET_DV), 0.02),
        'ev_gn_b': nrm((N_EVEN_LAYERS, RET_HEADS, RET_DV), 0.02),
        'ev_w_o': nrm((N_EVEN_LAYERS, EVEN_OUT, D_MODEL), BETA_INIT * EVEN_OUT ** -0.5),
        'od_w_in': nrm((N_ODD_LAYERS, D_MODEL, SB_IN), D_MODEL ** -0.5),
        'od_w_o': nrm((N_ODD_LAYERS, SB_HEADS * SB_DIM, D_MODEL), BETA_INIT * (SB_HEADS * SB_DIM) ** -0.5),
    }


def reference(x_prompt, x_sample, state_ret, cache_mla_ckv, cache_mla_kpe, cache_sb_k, cache_sb_v, page_table,
              ln_g, ln_b, ffn_w_gate, ffn_w_up, ffn_w_down,
              ev_w_in, ev_q_norm, ev_w_uq, ev_kv_norm, ev_w_ukv, ev_gn_g, ev_gn_b, ev_w_o,
              od_w_in, od_w_o):
    pos_p = jnp.arange(x_prompt.shape[1])
    pos_s = PAST_LEN + jnp.arange(x_sample.shape[1])
    log_gamma = retention_log_decay()
    xp, xs = x_prompt, x_sample
    ret_p, ret_s, ckv_p, ckv_s, kpe_p, kpe_s = [], [], [], [], [], []
    sbk_p, sbk_s, sbv_p, sbv_s = [], [], [], []
    for l in range(DEPTH):
        xp = ffn_half_step(xp, ffn_w_gate[l, 0], ffn_w_up[l, 0], ffn_w_down[l, 0], ln_g[l, 0], ln_b[l, 0])
        xs = ffn_half_step(xs, ffn_w_gate[l, 0], ffn_w_up[l, 0], ffn_w_down[l, 0], ln_g[l, 0], ln_b[l, 0])
        if l % 2 == 0:
            e = l // 2
            qr, kr, vr, ga, qn, qpe, ckv, kpe = even_inputs(xp, pos_p, ev_w_in[e], ev_q_norm[e], ev_w_uq[e], ev_kv_norm[e])
            o_r, S_p = retention_prompt(qr, kr, vr, log_gamma)
            mix_p = jnp.concatenate([retention_out(o_r, ga, ev_gn_g[e], ev_gn_b[e]),
                                     mla_prompt(qn, qpe, ckv, kpe, ev_w_ukv[e])], axis=-1) @ ev_w_o[e]
            ret_p.append(S_p)
            ckv_p.append(ckv)
            kpe_p.append(kpe)
            qr, kr, vr, ga, qn, qpe, ckv, kpe = even_inputs(xs, pos_s, ev_w_in[e], ev_q_norm[e], ev_w_uq[e], ev_kv_norm[e])
            o_r, S_s = retention_chunk(state_ret[e], qr, kr, vr, log_gamma)
            mix_s = jnp.concatenate([retention_out(o_r, ga, ev_gn_g[e], ev_gn_b[e]),
                                     mla_sample(qn, qpe, ckv, kpe, ev_w_ukv[e], cache_mla_ckv, cache_mla_kpe, e, page_table)],
                                    axis=-1) @ ev_w_o[e]
            ret_s.append(S_s)
            ckv_s.append(ckv)
            kpe_s.append(kpe)
        else:
            o = l // 2
            q, k, v = sb_project(xp, od_w_in[o])
            mix_p = sb_prompt(q, k, v) @ od_w_o[o]
            sbk_p.append(k)
            sbv_p.append(v)
            q, k, v = sb_project(xs, od_w_in[o])
            mix_s = sb_sample(q, k, v, cache_sb_k, cache_sb_v, o, page_table) @ od_w_o[o]
            sbk_s.append(k)
            sbv_s.append(v)
        xp = layer_norm(ALPHA * xp + mix_p, ln_g[l, 1], ln_b[l, 1])
        xs = layer_norm(ALPHA * xs + mix_s, ln_g[l, 1], ln_b[l, 1])
        xp = ffn_half_step(xp, ffn_w_gate[l, 1], ffn_w_up[l, 1], ffn_w_down[l, 1], ln_g[l, 2], ln_b[l, 2])
        xs = ffn_half_step(xs, ffn_w_gate[l, 1], ffn_w_up[l, 1], ffn_w_down[l, 1], ln_g[l, 2], ln_b[l, 2])
    return (xp, xs, jnp.stack(ret_p), jnp.stack(ret_s), jnp.stack(ckv_p), jnp.stack(ckv_s),
            jnp.stack(kpe_p), jnp.stack(kpe_s), jnp.stack(sbk_p), jnp.stack(sbk_s), jnp.stack(sbv_p), jnp.stack(sbv_s))
```

```python
import functools

import jax
import jax.numpy as jnp
import numpy as np
from jax import lax
from jax.experimental import pallas as pl
from jax.experimental.pallas import tpu as pltpu

F32 = jnp.float32
BF16 = jnp.bfloat16

DEPTH = 2
PAGE = 128
RET_HEADS = 8
RET_D = 128
MLA_HEADS = 8
MLA_NOPE = 128
MLA_ROPE = 64
MLA_V = 128
MLA_RANK = 512
MLA_SCALE = (MLA_NOPE + MLA_ROPE) ** -0.5
SB_HEADS = 16
SB_KV_HEADS = 4
SB_GROUP = SB_HEADS // SB_KV_HEADS
SB_DIM = 128
SB_SCALE = SB_DIM ** -0.5
ROPE_BASE = 10000.0
EPS = 1e-5
ALPHA = (2 * DEPTH) ** 0.25
RET_CHUNK = 128
NEG = -1e30

LANE = 128
VMEM_LIMIT = 56 * 1024 * 1024
ROW_TILES = (512, 256, 128, 64, 32, 16, 8)


def _pick(n, cands):
    for c in cands:
        if n % c == 0:
            return c
    raise ValueError(f"no tile for {n}")


def _cp(*sem):
    return pltpu.CompilerParams(dimension_semantics=sem, vmem_limit_bytes=VMEM_LIMIT)


def _dot(a, b):
    return jnp.dot(a, b, preferred_element_type=F32)


def _dot_nt(a, b):
    return lax.dot_general(a, b, (((1,), (1,)), ((), ())), preferred_element_type=F32)


def _layer_norm(y, g, b):
    mu = jnp.mean(y, axis=-1, keepdims=True)
    d = y - mu
    var = jnp.mean(d * d, axis=-1, keepdims=True)
    return d * lax.rsqrt(var + EPS) * g + b


def _rms_norm(y, g):
    return y * lax.rsqrt(jnp.mean(y * y, axis=-1, keepdims=True) + EPS) * g


def _half_swap(x):
    return pltpu.roll(x, LANE // 2, 1)


def _ffn_kernel(x_ref, wg_ref, wu_ref, wd_ref, g_ref, b_ref, o_ref, xb_ref, acc_ref, *, nf):
    f = pl.program_id(1)

    @pl.when(f == 0)
    def _():
        xb_ref[...] = x_ref[...].astype(BF16)
        acc_ref[...] = jnp.zeros_like(acc_ref)

    xb = xb_ref[...]
    g = _dot(xb, wg_ref[...])
    u = _dot(xb, wu_ref[...])
    a = (g * jax.nn.sigmoid(g) * u).astype(BF16)
    acc_ref[...] += _dot(a, wd_ref[...])

    @pl.when(f == nf - 1)
    def _():
        y = ALPHA * x_ref[...] + 0.5 * acc_ref[...]
        o_ref[...] = _layer_norm(y, g_ref[...], b_ref[...])


def _ffn_half_step(x, wg, wu, wd, g, b):
    m, d = x.shape
    ff = wg.shape[1]
    tm = _pick(m, ROW_TILES)
    tf = _pick(ff, (512, 256, 128))
    nf = ff // tf
    return pl.pallas_call(
        functools.partial(_ffn_kernel, nf=nf),
        grid=(m // tm, nf),
        in_specs=[
            pl.BlockSpec((tm, d), lambda i, f: (i, 0)),
            pl.BlockSpec((d, tf), lambda i, f: (0, f)),
            pl.BlockSpec((d, tf), lambda i, f: (0, f)),
            pl.BlockSpec((tf, d), lambda i, f: (f, 0)),
            pl.BlockSpec((1, d), lambda i, f: (0, 0)),
            pl.BlockSpec((1, d), lambda i, f: (0, 0)),
        ],
        out_specs=pl.BlockSpec((tm, d), lambda i, f: (i, 0)),
        out_shape=jax.ShapeDtypeStruct((m, d), F32),
        scratch_shapes=[pltpu.VMEM((tm, d), BF16), pltpu.VMEM((tm, d), F32)],
        compiler_params=_cp("parallel", "arbitrary"),
        name="ffn_half_step",
    )(x, wg, wu, wd, g.reshape(1, d), b.reshape(1, d))


def _out_ln_kernel(*refs, n_parts):
    a_refs = refs[:n_parts]
    w_refs = refs[n_parts:2 * n_parts]
    x_ref, g_ref, b_ref, o_ref = refs[2 * n_parts:]
    acc = _dot(a_refs[0][...].astype(BF16), w_refs[0][...])
    for a_ref, w_ref in zip(a_refs[1:], w_refs[1:]):
        acc += _dot(a_ref[...].astype(BF16), w_ref[...])
    y = ALPHA * x_ref[...] + acc
    o_ref[...] = _layer_norm(y, g_ref[...], b_ref[...])


def _out_proj_ln(parts, weights, x, g, b):
    m, d = x.shape
    tm = _pick(m, ROW_TILES)
    n = len(parts)
    in_specs = [pl.BlockSpec((tm, p.shape[1]), lambda i: (i, 0)) for p in parts]
    in_specs += [pl.BlockSpec(w.shape, lambda i: (0, 0)) for w in weights]
    in_specs += [pl.BlockSpec((tm, d), lambda i: (i, 0)),
                 pl.BlockSpec((1, d), lambda i: (0, 0)),
                 pl.BlockSpec((1, d), lambda i: (0, 0))]
    return pl.pallas_call(
        functools.partial(_out_ln_kernel, n_parts=n),
        grid=(m // tm,),
        in_specs=in_specs,
        out_specs=pl.BlockSpec((tm, d), lambda i: (i, 0)),
        out_shape=jax.ShapeDtypeStruct((m, d), F32),
        compiler_params=_cp("parallel"),
        name="out_proj_ln",
    )(*parts, *weights, x, g.reshape(1, d), b.reshape(1, d))


EV_BLOCK = 512
EV_Q_BLOCKS = (0, 2)
EV_K_BLOCKS = (2, 4)
EV_CQ_BLOCK = 8
EV_CKV_BLOCK = 9
EV_MAIN_COLS = 10 * EV_BLOCK


def _ev_proj_kernel(x_ref, w_ref, cr_ref, sr_ref, gq_ref, gkv_ref, o_ref, xb_ref):
    j = pl.program_id(1)

    @pl.when(j == 0)
    def _():
        xb_ref[...] = x_ref[...].astype(BF16)

    acc = _dot(xb_ref[...], w_ref[...])

    def rope(scale):
        c, s = cr_ref[...], sr_ref[...]
        for h in range(EV_BLOCK // LANE):
            blk = acc[:, h * LANE:(h + 1) * LANE]
            r = blk * c + _half_swap(blk) * s
            o_ref[:, h * LANE:(h + 1) * LANE] = r if scale is None else r * scale

    @pl.when(j < EV_Q_BLOCKS[1])
    def _():
        rope(None)

    @pl.when((j >= EV_K_BLOCKS[0]) & (j < EV_K_BLOCKS[1]))
    def _():
        rope(RET_D ** -0.5)

    @pl.when((j >= EV_K_BLOCKS[1]) & (j < EV_CQ_BLOCK))
    def _():
        o_ref[...] = acc

    @pl.when(j == EV_CQ_BLOCK)
    def _():
        o_ref[...] = _rms_norm(acc, gq_ref[...])

    @pl.when(j == EV_CKV_BLOCK)
    def _():
        o_ref[...] = _rms_norm(acc, gkv_ref[...])


def _ev_proj(x, w_main, cr, sr, gq, gkv):
    m, d = x.shape
    tm = _pick(m, ROW_TILES)
    nb = EV_MAIN_COLS // EV_BLOCK
    return pl.pallas_call(
        _ev_proj_kernel,
        grid=(m // tm, nb),
        in_specs=[
            pl.BlockSpec((tm, d), lambda i, j: (i, 0)),
            pl.BlockSpec((d, EV_BLOCK), lambda i, j: (0, j)),
            pl.BlockSpec((tm, LANE), lambda i, j: (i, 0)),
            pl.BlockSpec((tm, LANE), lambda i, j: (i, 0)),
            pl.BlockSpec((1, MLA_RANK), lambda i, j: (0, 0)),
            pl.BlockSpec((1, MLA_RANK), lambda i, j: (0, 0)),
        ],
        out_specs=pl.BlockSpec((tm, EV_BLOCK), lambda i, j: (i, j)),
        out_shape=jax.ShapeDtypeStruct((m, EV_MAIN_COLS), F32),
        scratch_shapes=[pltpu.VMEM((tm, d), BF16)],
        compiler_params=_cp("parallel", "arbitrary"),
        name="even_in_proj",
    )(x, w_main, cr, sr, gq.reshape(1, -1), gkv.reshape(1, -1))


def _kpe_kernel(x_ref, w_ref, t1_ref, t2_ref, o_ref):
    acc = _dot(x_ref[...].astype(BF16), w_ref[...])
    o_ref[...] = acc * t1_ref[...] + _half_swap(acc) * t2_ref[...]


def _kpe_proj(x, w_kpe, t1, t2):
    m, d = x.shape
    tm = _pick(m, ROW_TILES)
    return pl.pallas_call(
        _kpe_kernel,
        grid=(m // tm,),
        in_specs=[
            pl.BlockSpec((tm, d), lambda i: (i, 0)),
            pl.BlockSpec((d, LANE), lambda i: (0, 0)),
            pl.BlockSpec((tm, LANE), lambda i: (i, 0)),
            pl.BlockSpec((tm, LANE), lambda i: (i, 0)),
        ],
        out_specs=pl.BlockSpec((tm, LANE), lambda i: (i, 0)),
        out_shape=jax.ShapeDtypeStruct((m, LANE), F32),
        compiler_params=_cp("parallel"),
        name="kpe_proj",
    )(x, w_kpe, t1, t2)


MLA_QW = 2 * LANE


def _uq_kernel(cq_ref, w_ref, t1_ref, t2_ref, o_ref):
    acc = _dot(cq_ref[...].astype(BF16), w_ref[...])
    t1, t2 = t1_ref[...], t2_ref[...]
    for h in range(MLA_HEADS):
        c0 = h * MLA_QW
        o_ref[:, c0:c0 + LANE] = acc[:, c0:c0 + LANE].astype(BF16)
        x = acc[:, c0 + LANE:c0 + MLA_QW]
        o_ref[:, c0 + LANE:c0 + MLA_QW] = (x * t1 + _half_swap(x) * t2).astype(BF16)


def _uq_proj(proj, w_uq_ext, t1, t2):
    m = proj.shape[0]
    tm = _pick(m, ROW_TILES[:-1])
    n = MLA_HEADS * MLA_QW
    return pl.pallas_call(
        _uq_kernel,
        grid=(m // tm,),
        in_specs=[
            pl.BlockSpec((tm, MLA_RANK), lambda i: (i, EV_CQ_BLOCK)),
            pl.BlockSpec((MLA_RANK, n), lambda i: (0, 0)),
            pl.BlockSpec((tm, LANE), lambda i: (i, 0)),
            pl.BlockSpec((tm, LANE), lambda i: (i, 0)),
        ],
        out_specs=pl.BlockSpec((tm, n), lambda i: (i, 0)),
        out_shape=jax.ShapeDtypeStruct((m, n), BF16),
        compiler_params=_cp("parallel"),
        name="mla_q_up_proj",
    )(proj, w_uq_ext, t1, t2)


def _ukv_kernel(ckv_ref, w_ref, kpe_ref, k_ref, v_ref):
    acc = _dot(ckv_ref[...].astype(BF16), w_ref[...])
    kpe = kpe_ref[...].astype(BF16)
    for h in range(MLA_HEADS):
        c0 = h * MLA_QW
        k_ref[:, c0:c0 + LANE] = acc[:, c0:c0 + LANE].astype(BF16)
        k_ref[:, c0 + LANE:c0 + MLA_QW] = kpe
        v_ref[:, h * MLA_V:(h + 1) * MLA_V] = acc[:, c0 + LANE:c0 + MLA_QW].astype(BF16)


def _ukv_proj(proj, w_ukv, kpe_pad, mp):
    tm = _pick(mp, ROW_TILES[:-1])
    n = MLA_HEADS * MLA_QW
    return pl.pallas_call(
        _ukv_kernel,
        grid=(mp // tm,),
        in_specs=[
            pl.BlockSpec((tm, MLA_RANK), lambda i: (i, EV_CKV_BLOCK)),
            pl.BlockSpec((MLA_RANK, n), lambda i: (0, 0)),
            pl.BlockSpec((tm, LANE), lambda i: (i, 0)),
        ],
        out_specs=[pl.BlockSpec((tm, n), lambda i: (i, 0)),
                   pl.BlockSpec((tm, MLA_HEADS * MLA_V), lambda i: (i, 0))],
        out_shape=[jax.ShapeDtypeStruct((mp, n), BF16),
                   jax.ShapeDtypeStruct((mp, MLA_HEADS * MLA_V), BF16)],
        compiler_params=_cp("parallel"),
        name="mla_kv_up_proj",
    )(proj, w_ukv, kpe_pad)


def _ret_core(q, k, v, g, s0, dec, qd, kd, cd, gn_g, gn_b):
    qb, kb, vb = q.astype(BF16), k.astype(BF16), v.astype(BF16)
    s = _dot_nt(qb, kb) * dec
    o = _dot(s.astype(BF16), vb)
    o = o + _dot(qb, s0.astype(BF16)) * qd
    kk_t = (k * kd).T.astype(BF16)
    s_new = s0 * cd + _dot(kk_t, vb)
    on = _layer_norm(o, gn_g, gn_b)
    return on * (g * jax.nn.sigmoid(g)), s_new


def _ret_prompt_kernel(q_ref, k_ref, v_ref, g_ref, dec_ref, qd_ref, kd_ref, cd_ref, gg_ref, gb_ref,
                       o_ref, s_out_ref, s_ref, *, nc):
    c = pl.program_id(2)

    @pl.when(c == 0)
    def _():
        s_ref[...] = jnp.zeros_like(s_ref)

    out, s_new = _ret_core(q_ref[...], k_ref[...], v_ref[...], g_ref[...], s_ref[...], dec_ref[...],
                           qd_ref[...], kd_ref[...], cd_ref[...], gg_ref[...], gb_ref[...])
    o_ref[...] = out
    s_ref[...] = s_new

    @pl.when(c == nc - 1)
    def _():
        s_out_ref[...] = s_new


def _ret_prompt(proj, tabs, gn_g, gn_b, batch, seq):
    nc = seq // RET_CHUNK
    h_blocks = RET_HEADS

    def tok(off):
        return pl.BlockSpec((RET_CHUNK, RET_D), lambda b, h, c: (b * nc + c, off + h))

    def per_head(shape):
        return pl.BlockSpec((None,) + shape, lambda b, h, c: (h, 0, 0))

    dec, qd, kd, cd = tabs
    return pl.pallas_call(
        functools.partial(_ret_prompt_kernel, nc=nc),
        grid=(batch, RET_HEADS, nc),
        in_specs=[tok(0), tok(h_blocks), tok(2 * h_blocks), tok(3 * h_blocks),
                  per_head((RET_CHUNK, RET_CHUNK)), per_head((RET_CHUNK, LANE)), per_head((RET_CHUNK, LANE)),
                  per_head((1, LANE)), per_head((1, RET_D)), per_head((1, RET_D))],
        out_specs=[pl.BlockSpec((RET_CHUNK, RET_D), lambda b, h, c: (b * nc + c, h)),
                   pl.BlockSpec((None, None, RET_D, RET_D), lambda b, h, c: (b, h, 0, 0))],
        out_shape=[jax.ShapeDtypeStruct((batch * seq, RET_HEADS * RET_D), F32),
                   jax.ShapeDtypeStruct((batch, RET_HEADS, RET_D, RET_D), F32)],
        scratch_shapes=[pltpu.VMEM((RET_D, RET_D), F32)],
        compiler_params=_cp("parallel", "parallel", "arbitrary"),
        name="retention_prompt",
    )(proj, proj, proj, proj, dec, qd, kd, cd, gn_g[:, None, :], gn_b[:, None, :])


def _ret_sample_kernel(q_ref, k_ref, v_ref, g_ref, s0_ref, dec_ref, qd_ref, kd_ref, cd_ref, gg_ref, gb_ref,
                       o_ref, s_out_ref, *, ts):
    pad = jnp.zeros((RET_CHUNK - ts, RET_D), F32)

    def padded(ref, h):
        return jnp.concatenate([ref[:, h * RET_D:(h + 1) * RET_D], pad], axis=0)

    for h in range(RET_HEADS):
        out, s_new = _ret_core(padded(q_ref, h), padded(k_ref, h), padded(v_ref, h), padded(g_ref, h),
                               s0_ref[h], dec_ref[h], qd_ref[h], kd_ref[h], cd_ref[h], gg_ref[h], gb_ref[h])
        o_ref[:, h * RET_D:(h + 1) * RET_D] = out[:ts]
        s_out_ref[h] = s_new


def _ret_sample(proj, state, tabs, gn_g, gn_b, mp, bd, ts):
    width = RET_HEADS * RET_D
    row0 = mp // ts

    def tok(j):
        return pl.BlockSpec((ts, width), lambda b: (row0 + b, j))

    def const(a):
        return pl.BlockSpec(a.shape, lambda b: (0,) * a.ndim)

    dec, qd, kd, cd = tabs
    gg, gb = gn_g[:, None, :], gn_b[:, None, :]
    return pl.pallas_call(
        functools.partial(_ret_sample_kernel, ts=ts),
        grid=(bd,),
        in_specs=[tok(0), tok(1), tok(2), tok(3),
                  pl.BlockSpec((None, RET_HEADS, RET_D, RET_D), lambda b: (b, 0, 0, 0)),
                  const(dec), const(qd), const(kd), const(cd), const(gg), const(gb)],
        out_specs=[pl.BlockSpec((ts, width), lambda b: (b, 0)),
                   pl.BlockSpec((None, RET_HEADS, RET_D, RET_D), lambda b: (b, 0, 0, 0))],
        out_shape=[jax.ShapeDtypeStruct((bd * ts, width), F32),
                   jax.ShapeDtypeStruct(state.shape, F32)],
        compiler_params=_cp("parallel"),
        name="retention_sample",
    )(proj, proj, proj, proj, state, dec, qd, kd, cd, gg, gb)


def _ret_tables(chunk_len):
    h = jnp.arange(RET_HEADS, dtype=F32)
    lg = jnp.log(1.0 - jnp.exp2(-5.0 - h))
    idx = jnp.arange(RET_CHUNK, dtype=F32)
    diff = idx[:, None] - idx[None, :]
    dec = jnp.where(diff >= 0, jnp.exp(jnp.maximum(diff, 0.0)[None] * lg[:, None, None]), 0.0)
    qd = jnp.exp((idx + 1.0)[None, :] * lg[:, None])
    kd = jnp.exp((chunk_len - 1.0 - idx)[None, :] * lg[:, None])
    kd = jnp.where(idx[None, :] < chunk_len, kd, 0.0)
    cd = jnp.exp(chunk_len * lg)
    rep = lambda a: jnp.broadcast_to(a[..., None], a.shape + (LANE,))
    return dec, rep(qd), rep(kd), rep(cd[:, None])


def _mla_prompt_kernel(q_ref, k_ref, v_ref, o_ref, *, tq):
    i = pl.program_id(2)
    q = q_ref[...]

    def step(j, carry, masked):
        m, l, acc = carry
        rows = pl.ds(pl.multiple_of(j * tq, tq), tq)
        s = _dot_nt(q, k_ref[rows, :]) * MLA_SCALE
        if masked:
            r = lax.broadcasted_iota(jnp.int32, (tq, tq), 0)
            c = lax.broadcasted_iota(jnp.int32, (tq, tq), 1)
            s = jnp.where(c <= r, s, NEG)
        m_new = jnp.maximum(m, jnp.max(s, axis=1, keepdims=True))
        a = jnp.exp(m - m_new)
        p = jnp.exp(s - m_new)
        l = a * l + jnp.sum(p, axis=1, keepdims=True)
        acc = a * acc + _dot(p.astype(BF16), v_ref[rows, :])
        return m_new, l, acc

    init = (jnp.full((tq, 1), NEG, F32), jnp.zeros((tq, 1), F32), jnp.zeros((tq, MLA_V), F32))
    carry = lax.fori_loop(0, i, lambda j, c: step(j, c, False), init)
    m, l, acc = step(i, carry, True)
    o_ref[...] = acc / l


def _mla_prompt(q_cat, k_cat, v, batch, seq):
    tq = _pick(seq, (256, 128))
    nq = seq // tq
    return pl.pallas_call(
        functools.partial(_mla_prompt_kernel, tq=tq),
        grid=(batch, MLA_HEADS, nq),
        in_specs=[pl.BlockSpec((tq, MLA_QW), lambda b, h, i: (b * nq + i, h)),
                  pl.BlockSpec((seq, MLA_QW), lambda b, h, i: (b, h)),
                  pl.BlockSpec((seq, MLA_V), lambda b, h, i: (b, h))],
        out_specs=pl.BlockSpec((tq, MLA_V), lambda b, h, i: (b * nq + i, h)),
        out_shape=jax.ShapeDtypeStruct((batch * seq, MLA_HEADS * MLA_V), F32),
        compiler_params=_cp("parallel", "parallel", "arbitrary"),
        name="mla_prompt_attention",
    )(q_cat, k_cat, v)


def _qlat_kernel(q_ref, w_ref, o_ref):
    o_ref[...] = _dot(q_ref[...], w_ref[...])


def _q_latent(q_cat, w_uk_t, mp, ms):
    tm = _pick(ms, ROW_TILES[:-1])
    row0 = mp // tm
    return pl.pallas_call(
        _qlat_kernel,
        grid=(MLA_HEADS, ms // tm),
        in_specs=[pl.BlockSpec((tm, MLA_NOPE), lambda h, i: (row0 + i, 2 * h)),
                  pl.BlockSpec((None, MLA_NOPE, MLA_RANK), lambda h, i: (h, 0, 0))],
        out_specs=pl.BlockSpec((None, tm, MLA_RANK), lambda h, i: (h, i, 0)),
        out_shape=jax.ShapeDtypeStruct((MLA_HEADS, ms, MLA_RANK), F32),
        compiler_params=_cp("parallel", "parallel"),
        name="mla_q_latent",
    )(q_cat, w_uk_t)


def _mla_sample_kernel(pt_ref, ql_ref, qp_ref, ckvn_ref, kpen_ref, *refs, npg, ts):
    ckv_refs = refs[:npg]
    kpe_refs = refs[npg:2 * npg]
    o_ref, m_ref, l_ref, acc_ref = refs[2 * npg:]
    c = pl.program_id(1)
    nc = pl.num_programs(1)
    rows = MLA_HEADS * ts

    @pl.when(c == 0)
    def _():
        m_ref[...] = jnp.full_like(m_ref, NEG)
        l_ref[...] = jnp.zeros_like(l_ref)
        acc_ref[...] = jnp.zeros_like(acc_ref)

    ql = ql_ref[...].reshape(rows, MLA_RANK).astype(BF16)
    qp = qp_ref[...].astype(BF16)

    def update(s_list, v_list):
        s = jnp.concatenate(s_list, axis=1) if len(s_list) > 1 else s_list[0]
        m = m_ref[...]
        m_new = jnp.maximum(m, jnp.max(s, axis=1, keepdims=True))
        a = jnp.exp(m - m_new)
        p = jnp.exp(s - m_new)
        l_ref[...] = a * l_ref[...] + jnp.sum(p, axis=1, keepdims=True)
        pb = p.astype(BF16)
        acc = a * acc_ref[...]
        for j, vb in enumerate(v_list):
            acc += _dot(pb[:, j * PAGE:(j + 1) * PAGE], vb)
        acc_ref[...] = acc
        m_ref[...] = m_new

    s_list, v_list = [], []
    for j in range(npg):
        ckv = ckv_refs[j][...].astype(BF16)
        kpe = kpe_refs[j][...].astype(BF16)
        s_list.append((_dot_nt(ql, ckv) + _dot_nt(qp, kpe)) * MLA_SCALE)
        v_list.append(ckv)
    update(s_list, v_list)

    @pl.when(c == nc - 1)
    def _():
        ckv = jnp.concatenate([ckvn_ref[...], jnp.zeros((PAGE - ts, MLA_RANK), F32)], axis=0).astype(BF16)
        kpe = jnp.concatenate([kpen_ref[...], jnp.zeros((PAGE - ts, MLA_ROPE), F32)], axis=0).astype(BF16)
        s = (_dot_nt(ql, ckv) + _dot_nt(qp, kpe)) * MLA_SCALE
        t = lax.broadcasted_iota(jnp.int32, (rows, PAGE), 0) % ts
        key = lax.broadcasted_iota(jnp.int32, (rows, PAGE), 1)
        update([jnp.where(key <= t, s, NEG)], [ckv])
        o_ref[...] = acc_ref[...] / l_ref[...]


def _mla_sample(page_table_flat, q_lat, q_pe, ckv_new, kpe_new, cache_ckv, cache_kpe, bd, ts, n_pages):
    npg = _pick(n_pages, (8, 4, 2, 1))
    nchunk = n_pages // npg
    rows = MLA_HEADS * ts

    def page_spec(width, j):
        return pl.BlockSpec((None, PAGE, width), lambda b, c, pt: (pt[b * n_pages + c * npg + j], 0, 0))

    in_specs = [pl.BlockSpec((MLA_HEADS, ts, MLA_RANK), lambda b, c, pt: (0, b, 0)),
                pl.BlockSpec((None, rows, MLA_ROPE), lambda b, c, pt: (b, 0, 0)),
                pl.BlockSpec((None, ts, MLA_RANK), lambda b, c, pt: (b, 0, 0)),
                pl.BlockSpec((None, ts, MLA_ROPE), lambda b, c, pt: (b, 0, 0))]
    in_specs += [page_spec(MLA_RANK, j) for j in range(npg)]
    in_specs += [page_spec(MLA_ROPE, j) for j in range(npg)]
    return pl.pallas_call(
        functools.partial(_mla_sample_kernel, npg=npg, ts=ts),
        grid_spec=pltpu.PrefetchScalarGridSpec(
            num_scalar_prefetch=1,
            grid=(bd, nchunk),
            in_specs=in_specs,
            out_specs=pl.BlockSpec((None, rows, MLA_RANK), lambda b, c, pt: (b, 0, 0)),
            scratch_shapes=[pltpu.VMEM((rows, 1), F32), pltpu.VMEM((rows, 1), F32),
                            pltpu.VMEM((rows, MLA_RANK), F32)],
        ),
        out_shape=jax.ShapeDtypeStruct((bd, rows, MLA_RANK), F32),
        compiler_params=_cp("parallel", "arbitrary"),
        name="mla_sample_attention",
    )(page_table_flat, q_lat, q_pe, ckv_new, kpe_new, *([cache_ckv] * npg), *([cache_kpe] * npg))


def _uv_kernel(o_ref, w_ref, out_ref, *, ts):
    tb = o_ref.shape[0]
    o = o_ref[...].reshape(tb * ts, MLA_RANK).astype(BF16)
    out_ref[...] = _dot(o, w_ref[...])


def _mla_value_up(o_lat, w_uv, bd, ts):
    tb = _pick(bd, (64, 32, 16, 8, 4, 2, 1))
    return pl.pallas_call(
        functools.partial(_uv_kernel, ts=ts),
        grid=(MLA_HEADS, bd // tb),
        in_specs=[pl.BlockSpec((tb, None, ts, MLA_RANK), lambda h, i: (i, h, 0, 0)),
                  pl.BlockSpec((None, MLA_RANK, MLA_V), lambda h, i: (h, 0, 0))],
        out_specs=pl.BlockSpec((tb * ts, MLA_V), lambda h, i: (i, h)),
        out_shape=jax.ShapeDtypeStruct((bd * ts, MLA_HEADS * MLA_V), F32),
        compiler_params=_cp("parallel", "parallel"),
        name="mla_value_up_proj",
    )(o_lat, w_uv)


def _mm_kernel(x_ref, w_ref, o_ref, xb_ref):
    @pl.when(pl.program_id(1) == 0)
    def _():
        xb_ref[...] = x_ref[...].astype(BF16)

    o_ref[...] = _dot(xb_ref[...], w_ref[...])


def _matmul(x, w):
    m, d = x.shape
    n = w.shape[1]
    tm = _pick(m, ROW_TILES)
    tn = _pick(n, (512, 256, 128))
    return pl.pallas_call(
        _mm_kernel,
        grid=(m // tm, n // tn),
        in_specs=[pl.BlockSpec((tm, d), lambda i, j: (i, 0)),
                  pl.BlockSpec((d, tn), lambda i, j: (0, j))],
        out_specs=pl.BlockSpec((tm, tn), lambda i, j: (i, j)),
        out_shape=jax.ShapeDtypeStruct((m, n), F32),
        scratch_shapes=[pltpu.VMEM((tm, d), BF16)],
        compiler_params=_cp("parallel", "arbitrary"),
        name="sb_in_proj",
    )(x, w)


SB_Q_COLS = SB_HEADS * SB_DIM
SB_KV_COLS = SB_KV_HEADS * SB_DIM
SB_TK = 128


def _sb_block(z, mask, carry, tri_ones):
    r = z.shape[0]
    sp = jnp.maximum(z, 0.0) + jnp.log1p(jnp.exp(-jnp.abs(z)))
    if mask is not None:
        sp_m = jnp.where(mask, sp, 0.0)
    else:
        sp_m = sp
    hi = sp_m.astype(BF16)
    lo = (sp_m - hi.astype(F32)).astype(BF16)
    cs = _dot(jnp.concatenate([hi, lo], axis=0), tri_ones)
    cs = cs[:r] + cs[r:]
    a = jnp.exp(z - sp + carry - cs[:, :SB_TK])
    if mask is not None:
        a = jnp.where(mask, a, 0.0)
    return a, carry - cs[:, SB_TK:]


def _sb_prompt_kernel(q_ref, k_ref, v_ref, to_ref, o_ref, *, tq):
    i = pl.program_id(2)
    rows = SB_GROUP * tq
    q = jnp.concatenate([q_ref[:, g * SB_DIM:(g + 1) * SB_DIM] for g in range(SB_GROUP)], axis=0).astype(BF16)
    tri_ones = to_ref[...]

    def step(j, carry, acc, mask):
        ks = pl.ds(pl.multiple_of(j * SB_TK, SB_TK), SB_TK)
        z = _dot_nt(q, k_ref[ks, :].astype(BF16)) * SB_SCALE
        a, carry = _sb_block(z, mask, carry, tri_ones)
        return carry, acc + _dot(a.astype(BF16), v_ref[ks, :].astype(BF16))

    t = lax.broadcasted_iota(jnp.int32, (rows, SB_TK), 0) % tq
    key = lax.broadcasted_iota(jnp.int32, (rows, SB_TK), 1)
    carry, acc = step(i, jnp.zeros((rows, SB_TK), F32), jnp.zeros((rows, SB_DIM), F32), key < t)
    carry, acc = lax.fori_loop(0, i, lambda n, c: step(i - 1 - n, c[0], c[1], None), (carry, acc))
    for g in range(SB_GROUP):
        o_ref[:, g * SB_DIM:(g + 1) * SB_DIM] = acc[g * tq:(g + 1) * tq]


def _sb_prompt(proj, tri_ones, batch, seq):
    tq = SB_TK
    nq = seq // tq
    gw = SB_GROUP * SB_DIM
    kb0 = SB_Q_COLS // SB_DIM
    vb0 = kb0 + SB_KV_HEADS
    return pl.pallas_call(
        functools.partial(_sb_prompt_kernel, tq=tq),
        grid=(batch, SB_KV_HEADS, nq),
        in_specs=[pl.BlockSpec((tq, gw), lambda b, h, i: (b * nq + i, h)),
                  pl.BlockSpec((seq, SB_DIM), lambda b, h, i: (b, kb0 + h)),
                  pl.BlockSpec((seq, SB_DIM), lambda b, h, i: (b, vb0 + h)),
                  pl.BlockSpec((SB_TK, 2 * SB_TK), lambda b, h, i: (0, 0))],
        out_specs=pl.BlockSpec((tq, gw), lambda b, h, i: (b * nq + i, h)),
        out_shape=jax.ShapeDtypeStruct((batch * seq, SB_Q_COLS), F32),
        compiler_params=_cp("parallel", "parallel", "arbitrary"),
        name="sb_prompt_attention",
    )(proj, proj, proj, tri_ones)


def _sb_sample_kernel(pt_ref, q_ref, kn_ref, vn_ref, to_ref, *refs, npg, ts):
    k_refs = refs[:npg]
    v_refs = refs[npg:2 * npg]
    o_ref, carry_ref, acc_ref = refs[2 * npg:]
    c = pl.program_id(1)
    nc = pl.num_programs(1)
    hrows = SB_GROUP * ts
    rows = SB_KV_HEADS * hrows
    q = jnp.concatenate([q_ref[:, h * SB_DIM:(h + 1) * SB_DIM] for h in range(SB_HEADS)], axis=0).astype(BF16)
    tri_ones = to_ref[...]

    def block(kf, vf, mask, carry, acc):
        kb, vb = kf.astype(BF16), vf.astype(BF16)
        z = jnp.concatenate(
            [_dot_nt(q[h * hrows:(h + 1) * hrows], kb[:, h * SB_DIM:(h + 1) * SB_DIM]) for h in range(SB_KV_HEADS)],
            axis=0) * SB_SCALE
        a, carry = _sb_block(z, mask, carry, tri_ones)
        ab = a.astype(BF16)
        upd = jnp.concatenate(
            [_dot(ab[h * hrows:(h + 1) * hrows], vb[:, h * SB_DIM:(h + 1) * SB_DIM]) for h in range(SB_KV_HEADS)],
            axis=0)
        return carry, acc + upd

    @pl.when(c == 0)
    def _():
        zpad = jnp.zeros((PAGE - ts, SB_KV_COLS), F32)
        t = lax.broadcasted_iota(jnp.int32, (rows, SB_TK), 0) % ts
        key = lax.broadcasted_iota(jnp.int32, (rows, SB_TK), 1)
        carry, acc = block(jnp.concatenate([kn_ref[...], zpad], axis=0),
                           jnp.concatenate([vn_ref[...], zpad], axis=0),
                           key < t, jnp.zeros((rows, SB_TK), F32), jnp.zeros((rows, SB_DIM), F32))
        carry_ref[...] = carry
        acc_ref[...] = acc

    carry, acc = carry_ref[...], acc_ref[...]
    for j in range(npg):
        carry, acc = block(k_refs[j][...], v_refs[j][...], None, carry, acc)
    carry_ref[...] = carry
    acc_ref[...] = acc

    @pl.when(c == nc - 1)
    def _():
        for h in range(SB_HEADS):
            o_ref[:, h * SB_DIM:(h + 1) * SB_DIM] = acc[h * ts:(h + 1) * ts]


def _sb_sample(page_table_flat, proj, tri_ones, cache_k, cache_v, mp, bd, ts, n_pages):
    npg = _pick(n_pages, (8, 4, 2, 1))
    nchunk = n_pages // npg
    row0 = mp // ts
    rows = SB_HEADS * ts
    kb0 = SB_Q_COLS // SB_KV_COLS

    def page_spec(j):
        return pl.BlockSpec((None, PAGE, SB_KV_COLS),
                            lambda b, c, pt: (pt[b * n_pages + n_pages - 1 - (c * npg + j)], 0, 0))

    in_specs = [pl.BlockSpec((ts, SB_Q_COLS), lambda b, c, pt: (row0 + b, 0)),
                pl.BlockSpec((ts, SB_KV_COLS), lambda b, c, pt: (row0 + b, kb0)),
                pl.BlockSpec((ts, SB_KV_COLS), lambda b, c, pt: (row0 + b, kb0 + 1)),
                pl.BlockSpec((SB_TK, 2 * SB_TK), lambda b, c, pt: (0, 0))]
    in_specs += [page_spec(j) for j in range(npg)] * 2
    return pl.pallas_call(
        functools.partial(_sb_sample_kernel, npg=npg, ts=ts),
        grid_spec=pltpu.PrefetchScalarGridSpec(
            num_scalar_prefetch=1,
            grid=(bd, nchunk),
            in_specs=in_specs,
            out_specs=pl.BlockSpec((ts, SB_Q_COLS), lambda b, c, pt: (b, 0)),
            scratch_shapes=[pltpu.VMEM((rows, SB_TK), F32), pltpu.VMEM((rows, SB_DIM), F32)],
        ),
        out_shape=jax.ShapeDtypeStruct((bd * ts, SB_Q_COLS), F32),
        compiler_params=_cp("parallel", "arbitrary"),
        name="sb_sample_attention",
    )(page_table_flat, proj, proj, proj, tri_ones, *([cache_k] * npg), *([cache_v] * npg))


def _rope_tables(pos):
    p = pos.astype(F32)[:, None]
    inv_r = 1.0 / (ROPE_BASE ** jnp.linspace(0.0, 1.0, RET_D // 2, dtype=F32))
    ang = p * inv_r[None, :]
    c, s = jnp.cos(ang), jnp.sin(ang)
    cr = jnp.concatenate([c, c], axis=1)
    sr = jnp.concatenate([-s, s], axis=1)
    inv_m = 1.0 / (ROPE_BASE ** (jnp.arange(0, MLA_ROPE, 2, dtype=F32) / MLA_ROPE))
    ang = p * inv_m[None, :]
    c, s = jnp.cos(ang), jnp.sin(ang)
    z = jnp.zeros((pos.shape[0], LANE - MLA_ROPE), F32)
    t1 = jnp.concatenate([c, c, z], axis=1)
    t2 = jnp.concatenate([-s, s, z], axis=1)
    return cr, sr, t1, t2


def _swap_halves(w):
    half = w.shape[-1] // 2
    return jnp.concatenate([w[..., half:], w[..., :half]], axis=-1)


def kernel(x_prompt, x_sample, state_ret, cache_mla_ckv, cache_mla_kpe, cache_sb_k, cache_sb_v, page_table,
           ln_g, ln_b, ffn_w_gate, ffn_w_up, ffn_w_down,
           ev_w_in, ev_q_norm, ev_w_uq, ev_kv_norm, ev_w_ukv, ev_gn_g, ev_gn_b, ev_w_o,
           od_w_in, od_w_o):
    batch, seq, d = x_prompt.shape
    bd, ts, _ = x_sample.shape
    n_pages = page_table.shape[1]
    past = n_pages * PAGE
    mp, ms = batch * seq, bd * ts
    assert seq % RET_CHUNK == 0 and ts <= 8 and mp % ts == 0

    x = jnp.concatenate([x_prompt.reshape(mp, d), x_sample.reshape(ms, d)], axis=0)
    pos = jnp.concatenate([jnp.tile(jnp.arange(seq), batch), jnp.tile(past + jnp.arange(ts), bd)])
    cr, sr, t1, t2 = _rope_tables(pos)
    pt_flat = page_table.reshape(-1)
    tri = (jnp.arange(SB_TK)[:, None] > jnp.arange(SB_TK)[None, :]).astype(BF16)
    tri_ones = jnp.concatenate([tri, jnp.ones((SB_TK, SB_TK), BF16)], axis=1)
    tabs_p = _ret_tables(float(RET_CHUNK))
    tabs_s = _ret_tables(float(ts))

    outs = {k: [] for k in ("ret_p", "ret_s", "ckv_p", "ckv_s", "kpe_p", "kpe_s", "sbk_p", "sbk_s", "sbv_p", "sbv_s")}
    for l in range(DEPTH):
        x = _ffn_half_step(x, ffn_w_gate[l, 0].astype(BF16), ffn_w_up[l, 0].astype(BF16),
                           ffn_w_down[l, 0].astype(BF16), ln_g[l, 0], ln_b[l, 0])
        if l % 2 == 0:
            e = l // 2
            w_in = ev_w_in[e]
            w_main = w_in[:, :EV_MAIN_COLS].astype(BF16)
            w_kpe = w_in[:, EV_MAIN_COLS:]
            w_kpe = jnp.concatenate([w_kpe, _swap_halves(w_kpe)], axis=1).astype(BF16)
            proj = _ev_proj(x, w_main, cr, sr, ev_q_norm[e], ev_kv_norm[e])
            kpe_pad = _kpe_proj(x, w_kpe, t1, t2)

            w_uq = ev_w_uq[e].reshape(MLA_RANK, MLA_HEADS, MLA_NOPE + MLA_ROPE)
            w_uq_ext = jnp.concatenate([w_uq, _swap_halves(w_uq[..., MLA_NOPE:])], axis=-1)
            q_cat = _uq_proj(proj, w_uq_ext.reshape(MLA_RANK, MLA_HEADS * MLA_QW).astype(BF16), t1, t2)

            w_ukv = ev_w_ukv[e]
            k_cat, v_p = _ukv_proj(proj, w_ukv.astype(BF16), kpe_pad, mp)

            ret_out_p, s_p = _ret_prompt(proj, tabs_p, ev_gn_g[e], ev_gn_b[e], batch, seq)
            ret_out_s, s_s = _ret_sample(proj, state_ret[e], tabs_s, ev_gn_g[e], ev_gn_b[e], mp, bd, ts)
            mla_out_p = _mla_prompt(q_cat, k_cat, v_p, batch, seq)

            w3 = w_ukv.reshape(MLA_RANK, MLA_HEADS, MLA_NOPE + MLA_V)
            w_uk_t = w3[..., :MLA_NOPE].transpose(1, 2, 0).astype(BF16)
            w_uv = w3[..., MLA_NOPE:].transpose(1, 0, 2).astype(BF16)
            q_lat = _q_latent(q_cat, w_uk_t, mp, ms)
            q_pe_s = q_cat[mp:].reshape(bd, ts, MLA_HEADS, MLA_QW)[..., MLA_NOPE:MLA_NOPE + MLA_ROPE]
            q_pe_s = q_pe_s.transpose(0, 2, 1, 3).reshape(bd, MLA_HEADS * ts, MLA_ROPE)
            ckv = proj[:, EV_CKV_BLOCK * EV_BLOCK:(EV_CKV_BLOCK + 1) * EV_BLOCK]
            kpe = kpe_pad[:, :MLA_ROPE]
            ckv_s = ckv[mp:].reshape(bd, ts, MLA_RANK)
            kpe_s = kpe[mp:].reshape(bd, ts, MLA_ROPE)
            o_lat = _mla_sample(pt_flat, q_lat, q_pe_s, ckv_s, kpe_s, cache_mla_ckv[e], cache_mla_kpe[e],
                                bd, ts, n_pages)
            mla_out_s = _mla_value_up(o_lat.reshape(bd, MLA_HEADS, ts, MLA_RANK), w_uv, bd, ts)

            w_o = ev_w_o[e].astype(BF16)
            n_ret = RET_HEADS * RET_D
            x = _out_proj_ln([jnp.concatenate([ret_out_p, ret_out_s], axis=0),
                              jnp.concatenate([mla_out_p, mla_out_s], axis=0)],
                             [w_o[:n_ret], w_o[n_ret:]], x, ln_g[l, 1], ln_b[l, 1])
            outs["ret_p"].append(s_p)
            outs["ret_s"].append(s_s)
            outs["ckv_p"].append(ckv[:mp].reshape(batch, seq, MLA_RANK))
            outs["ckv_s"].append(ckv_s)
            outs["kpe_p"].append(kpe[:mp].reshape(batch, seq, MLA_ROPE))
            outs["kpe_s"].append(kpe_s)
        else:
            o = l // 2
            proj = _matmul(x, od_w_in[o].astype(BF16))
            n_kv = cache_sb_k.shape[1]
            cache_k = cache_sb_k[o].reshape(n_kv, PAGE, SB_KV_COLS)
            cache_v = cache_sb_v[o].reshape(n_kv, PAGE, SB_KV_COLS)
            att_p = _sb_prompt(proj, tri_ones, batch, seq)
            att_s = _sb_sample(pt_flat, proj, tri_ones, cache_k, cache_v, mp, bd, ts, n_pages)
            x = _out_proj_ln([jnp.concatenate([att_p, att_s], axis=0)], [od_w_o[o].astype(BF16)],
                             x, ln_g[l, 1], ln_b[l, 1])
            k = proj[:, SB_Q_COLS:SB_Q_COLS + SB_KV_COLS]
            v = proj[:, SB_Q_COLS + SB_KV_COLS:]
            outs["sbk_p"].append(k[:mp].reshape(batch, seq, SB_KV_HEADS, SB_DIM))
            outs["sbk_s"].append(k[mp:].reshape(bd, ts, SB_KV_HEADS, SB_DIM))
            outs["sbv_p"].append(v[:mp].reshape(batch, seq, SB_KV_HEADS, SB_DIM))
            outs["sbv_s"].append(v[mp:].reshape(bd, ts, SB_KV_HEADS, SB_DIM))
        x = _ffn_half_step(x, ffn_w_gate[l, 1].astype(BF16), ffn_w_up[l, 1].astype(BF16),
                           ffn_w_down[l, 1].astype(BF16), ln_g[l, 2], ln_b[l, 2])

    y_p = x[:mp].reshape(batch, seq, d)
    y_s = x[mp:].reshape(bd, ts, d)
    return (y_p, y_s, jnp.stack(outs["ret_p"]), jnp.stack(outs["ret_s"]),
            jnp.stack(outs["ckv_p"]), jnp.stack(outs["ckv_s"]), jnp.stack(outs["kpe_p"]), jnp.stack(outs["kpe_s"]),
            jnp.stack(outs["sbk_p"]), jnp.stack(outs["sbk_s"]), jnp.stack(outs["sbv_p"]), jnp.stack(outs["sbv_s"]))
```

```python
import functools

import jax
import jax.numpy as jnp
import numpy as np
from jax import lax
from jax.experimental import pallas as pl
from jax.experimental.pallas import tpu as pltpu

F32 = jnp.float32
BF16 = jnp.bfloat16

DEPTH = 2
PAGE = 128
RET_HEADS = 8
RET_D = 128
MLA_HEADS = 8
MLA_NOPE = 128
MLA_ROPE = 64
MLA_V = 128
MLA_RANK = 512
MLA_SCALE = (MLA_NOPE + MLA_ROPE) ** -0.5
SB_HEADS = 16
SB_KV_HEADS = 4
SB_GROUP = SB_HEADS // SB_KV_HEADS
SB_DIM = 128
SB_SCALE = SB_DIM ** -0.5
ROPE_BASE = 10000.0
EPS = 1e-5
ALPHA = (2 * DEPTH) ** 0.25
RET_CHUNK = 128
NEG = -1e30

LANE = 128
VMEM_LIMIT = 56 * 1024 * 1024
ROW_TILES = (512, 256, 128, 64, 32, 16, 8)


def _pick(n, cands):
    for c in cands:
        if n % c == 0:
            return c
    raise ValueError(f"no tile for {n}")


def _cp(*sem):
    return pltpu.CompilerParams(dimension_semantics=sem, vmem_limit_bytes=VMEM_LIMIT)


def _dot(a, b):
    return jnp.dot(a, b, preferred_element_type=F32)


def _dot_nt(a, b):
    return lax.dot_general(a, b, (((1,), (1,)), ((), ())), preferred_element_type=F32)


def _layer_norm(y, g, b):
    mu = jnp.mean(y, axis=-1, keepdims=True)
    d = y - mu
    var = jnp.mean(d * d, axis=-1, keepdims=True)
    return d * lax.rsqrt(var + EPS) * g + b


def _rms_norm(y, g):
    return y * lax.rsqrt(jnp.mean(y * y, axis=-1, keepdims=True) + EPS) * g


def _half_swap(x):
    return pltpu.roll(x, LANE // 2, 1)


def _ffn_kernel(x_ref, wg_ref, wu_ref, wd_ref, g_ref, b_ref, o_ref, xb_ref, acc_ref, *, nf):
    f = pl.program_id(1)

    @pl.when(f == 0)
    def _():
        xb_ref[...] = x_ref[...].astype(BF16)
        acc_ref[...] = jnp.zeros_like(acc_ref)

    xb = xb_ref[...]
    g = _dot(xb, wg_ref[...])
    u = _dot(xb, wu_ref[...])
    a = (g * jax.nn.sigmoid(g) * u).astype(BF16)
    acc_ref[...] += _dot(a, wd_ref[...])

    @pl.when(f == nf - 1)
    def _():
        y = ALPHA * x_ref[...] + 0.5 * acc_ref[...]
        o_ref[...] = _layer_norm(y, g_ref[...], b_ref[...])


def _ffn_half_step(x, wg, wu, wd, layer, half, g, b):
    m, d = x.shape
    ff = wg.shape[-1]
    tm = _pick(m, ROW_TILES)
    tf = _pick(ff, (512, 256, 128))
    nf = ff // tf
    return pl.pallas_call(
        functools.partial(_ffn_kernel, nf=nf),
        grid=(m // tm, nf),
        in_specs=[
            pl.BlockSpec((tm, d), lambda i, f: (i, 0)),
            pl.BlockSpec((None, None, d, tf), lambda i, f: (layer, half, 0, f)),
            pl.BlockSpec((None, None, d, tf), lambda i, f: (layer, half, 0, f)),
            pl.BlockSpec((None, None, tf, d), lambda i, f: (layer, half, f, 0)),
            pl.BlockSpec((1, d), lambda i, f: (0, 0)),
            pl.BlockSpec((1, d), lambda i, f: (0, 0)),
        ],
        out_specs=pl.BlockSpec((tm, d), lambda i, f: (i, 0)),
        out_shape=jax.ShapeDtypeStruct((m, d), F32),
        scratch_shapes=[pltpu.VMEM((tm, d), BF16), pltpu.VMEM((tm, d), F32)],
        compiler_params=_cp("parallel", "arbitrary"),
        name="ffn_half_step",
    )(x, wg, wu, wd, g.reshape(1, d), b.reshape(1, d))


def _out_ln_kernel(*refs, n_parts):
    a_refs = refs[:n_parts]
    w_refs = refs[n_parts:2 * n_parts]
    x_ref, g_ref, b_ref, o_ref = refs[2 * n_parts:]
    acc = _dot(a_refs[0][...].astype(BF16), w_refs[0][...])
    for a_ref, w_ref in zip(a_refs[1:], w_refs[1:]):
        acc += _dot(a_ref[...].astype(BF16), w_ref[...])
    y = ALPHA * x_ref[...] + acc
    o_ref[...] = _layer_norm(y, g_ref[...], b_ref[...])


def _out_proj_ln(parts, weights, x, g, b):
    m, d = x.shape
    tm = _pick(m, ROW_TILES)
    n = len(parts)
    in_specs = [pl.BlockSpec((tm, p.shape[1]), lambda i: (i, 0)) for p in parts]
    in_specs += [pl.BlockSpec(w.shape, lambda i: (0, 0)) for w in weights]
    in_specs += [pl.BlockSpec((tm, d), lambda i: (i, 0)),
                 pl.BlockSpec((1, d), lambda i: (0, 0)),
                 pl.BlockSpec((1, d), lambda i: (0, 0))]
    return pl.pallas_call(
        functools.partial(_out_ln_kernel, n_parts=n),
        grid=(m // tm,),
        in_specs=in_specs,
        out_specs=pl.BlockSpec((tm, d), lambda i: (i, 0)),
        out_shape=jax.ShapeDtypeStruct((m, d), F32),
        compiler_params=_cp("parallel"),
        name="out_proj_ln",
    )(*parts, *weights, x, g.reshape(1, d), b.reshape(1, d))


EV_BLOCK = 512
EV_Q_BLOCKS = (0, 2)
EV_K_BLOCKS = (2, 4)
EV_CQ_BLOCK = 8
EV_CKV_BLOCK = 9
EV_MAIN_COLS = 10 * EV_BLOCK


def _ev_proj_kernel(x_ref, w_ref, cr_ref, sr_ref, gq_ref, gkv_ref, o_ref, xb_ref):
    j = pl.program_id(1)

    @pl.when(j == 0)
    def _():
        xb_ref[...] = x_ref[...].astype(BF16)

    acc = _dot(xb_ref[...], w_ref[...])

    def rope(scale):
        c, s = cr_ref[...], sr_ref[...]
        for h in range(EV_BLOCK // LANE):
            blk = acc[:, h * LANE:(h + 1) * LANE]
            r = blk * c + _half_swap(blk) * s
            o_ref[:, h * LANE:(h + 1) * LANE] = r if scale is None else r * scale

    @pl.when(j < EV_Q_BLOCKS[1])
    def _():
        rope(None)

    @pl.when((j >= EV_K_BLOCKS[0]) & (j < EV_K_BLOCKS[1]))
    def _():
        rope(RET_D ** -0.5)

    @pl.when((j >= EV_K_BLOCKS[1]) & (j < EV_CQ_BLOCK))
    def _():
        o_ref[...] = acc

    @pl.when(j == EV_CQ_BLOCK)
    def _():
        o_ref[...] = _rms_norm(acc, gq_ref[...])

    @pl.when(j == EV_CKV_BLOCK)
    def _():
        o_ref[...] = _rms_norm(acc, gkv_ref[...])


def _ev_proj(x, w_main, cr, sr, gq, gkv):
    m, d = x.shape
    tm = _pick(m, ROW_TILES)
    nb = EV_MAIN_COLS // EV_BLOCK
    return pl.pallas_call(
        _ev_proj_kernel,
        grid=(m // tm, nb),
        in_specs=[
            pl.BlockSpec((tm, d), lambda i, j: (i, 0)),
            pl.BlockSpec((d, EV_BLOCK), lambda i, j: (0, j)),
            pl.BlockSpec((tm, LANE), lambda i, j: (i, 0)),
            pl.BlockSpec((tm, LANE), lambda i, j: (i, 0)),
            pl.BlockSpec((1, MLA_RANK), lambda i, j: (0, 0)),
            pl.BlockSpec((1, MLA_RANK), lambda i, j: (0, 0)),
        ],
        out_specs=pl.BlockSpec((tm, EV_BLOCK), lambda i, j: (i, j)),
        out_shape=jax.ShapeDtypeStruct((m, EV_MAIN_COLS), F32),
        scratch_shapes=[pltpu.VMEM((tm, d), BF16)],
        compiler_params=_cp("parallel", "arbitrary"),
        name="even_in_proj",
    )(x, w_main, cr, sr, gq.reshape(1, -1), gkv.reshape(1, -1))


def _kpe_kernel(x_ref, w_ref, t1_ref, t2_ref, o_ref):
    acc = _dot(x_ref[...].astype(BF16), w_ref[...])
    o_ref[...] = acc * t1_ref[...] + _half_swap(acc) * t2_ref[...]


def _kpe_proj(x, w_kpe, t1, t2):
    m, d = x.shape
    tm = _pick(m, ROW_TILES)
    return pl.pallas_call(
        _kpe_kernel,
        grid=(m // tm,),
        in_specs=[
            pl.BlockSpec((tm, d), lambda i: (i, 0)),
            pl.BlockSpec((d, LANE), lambda i: (0, 0)),
            pl.BlockSpec((tm, LANE), lambda i: (i, 0)),
            pl.BlockSpec((tm, LANE), lambda i: (i, 0)),
        ],
        out_specs=pl.BlockSpec((tm, LANE), lambda i: (i, 0)),
        out_shape=jax.ShapeDtypeStruct((m, LANE), F32),
        compiler_params=_cp("parallel"),
        name="kpe_proj",
    )(x, w_kpe, t1, t2)


MLA_QW = 2 * LANE


def _uq_kernel(cq_ref, w_ref, t1_ref, t2_ref, o_ref):
    acc = _dot(cq_ref[...].astype(BF16), w_ref[...])
    t1, t2 = t1_ref[...], t2_ref[...]
    for h in range(MLA_HEADS):
        c0 = h * MLA_QW
        o_ref[:, c0:c0 + LANE] = acc[:, c0:c0 + LANE].astype(BF16)
        x = acc[:, c0 + LANE:c0 + MLA_QW]
        o_ref[:, c0 + LANE:c0 + MLA_QW] = (x * t1 + _half_swap(x) * t2).astype(BF16)


def _uq_proj(proj, w_uq_ext, t1, t2):
    m = proj.shape[0]
    tm = _pick(m, ROW_TILES[:-1])
    n = MLA_HEADS * MLA_QW
    return pl.pallas_call(
        _uq_kernel,
        grid=(m // tm,),
        in_specs=[
            pl.BlockSpec((tm, MLA_RANK), lambda i: (i, EV_CQ_BLOCK)),
            pl.BlockSpec((MLA_RANK, n), lambda i: (0, 0)),
            pl.BlockSpec((tm, LANE), lambda i: (i, 0)),
            pl.BlockSpec((tm, LANE), lambda i: (i, 0)),
        ],
        out_specs=pl.BlockSpec((tm, n), lambda i: (i, 0)),
        out_shape=jax.ShapeDtypeStruct((m, n), BF16),
        compiler_params=_cp("parallel"),
        name="mla_q_up_proj",
    )(proj, w_uq_ext, t1, t2)


def _ukv_kernel(ckv_ref, w_ref, kpe_ref, k_ref, v_ref):
    acc = _dot(ckv_ref[...].astype(BF16), w_ref[...])
    kpe = kpe_ref[...].astype(BF16)
    for h in range(MLA_HEADS):
        c0 = h * MLA_QW
        k_ref[:, c0:c0 + LANE] = acc[:, c0:c0 + LANE].astype(BF16)
        k_ref[:, c0 + LANE:c0 + MLA_QW] = kpe
        v_ref[:, h * MLA_V:(h + 1) * MLA_V] = acc[:, c0 + LANE:c0 + MLA_QW].astype(BF16)


def _ukv_proj(proj, w_ukv, kpe_pad, mp):
    tm = _pick(mp, ROW_TILES[:-1])
    n = MLA_HEADS * MLA_QW
    return pl.pallas_call(
        _ukv_kernel,
        grid=(mp // tm,),
        in_specs=[
            pl.BlockSpec((tm, MLA_RANK), lambda i: (i, EV_CKV_BLOCK)),
            pl.BlockSpec((MLA_RANK, n), lambda i: (0, 0)),
            pl.BlockSpec((tm, LANE), lambda i: (i, 0)),
        ],
        out_specs=[pl.BlockSpec((tm, n), lambda i: (i, 0)),
                   pl.BlockSpec((tm, MLA_HEADS * MLA_V), lambda i: (i, 0))],
        out_shape=[jax.ShapeDtypeStruct((mp, n), BF16),
                   jax.ShapeDtypeStruct((mp, MLA_HEADS * MLA_V), BF16)],
        compiler_params=_cp("parallel"),
        name="mla_kv_up_proj",
    )(proj, w_ukv, kpe_pad)


def _ret_core(q, k, v, g, s0, dec, qd, kd, cd, gn_g, gn_b):
    qb, kb, vb = q.astype(BF16), k.astype(BF16), v.astype(BF16)
    s = _dot_nt(qb, kb) * dec
    o = _dot(s.astype(BF16), vb)
    o = o + _dot(qb, s0.astype(BF16)) * qd
    kk_t = (k * kd).T.astype(BF16)
    s_new = s0 * cd + _dot(kk_t, vb)
    on = _layer_norm(o, gn_g, gn_b)
    return on * (g * jax.nn.sigmoid(g)), s_new


def _ret_prompt_kernel(q_ref, k_ref, v_ref, g_ref, dec_ref, qd_ref, kd_ref, cd_ref, gg_ref, gb_ref,
                       o_ref, s_out_ref, s_ref, *, nc):
    c = pl.program_id(2)

    @pl.when(c == 0)
    def _():
        s_ref[...] = jnp.zeros_like(s_ref)

    out, s_new = _ret_core(q_ref[...], k_ref[...], v_ref[...], g_ref[...], s_ref[...], dec_ref[...],
                           qd_ref[...], kd_ref[...], cd_ref[...], gg_ref[...], gb_ref[...])
    o_ref[...] = out
    s_ref[...] = s_new

    @pl.when(c == nc - 1)
    def _():
        s_out_ref[...] = s_new


def _ret_prompt(proj, tabs, gn_g, gn_b, batch, seq):
    nc = seq // RET_CHUNK
    h_blocks = RET_HEADS

    def tok(off):
        return pl.BlockSpec((RET_CHUNK, RET_D), lambda b, h, c: (b * nc + c, off + h))

    def per_head(shape):
        return pl.BlockSpec((None,) + shape, lambda b, h, c: (h, 0, 0))

    dec, qd, kd, cd = tabs
    return pl.pallas_call(
        functools.partial(_ret_prompt_kernel, nc=nc),
        grid=(batch, RET_HEADS, nc),
        in_specs=[tok(0), tok(h_blocks), tok(2 * h_blocks), tok(3 * h_blocks),
                  per_head((RET_CHUNK, RET_CHUNK)), per_head((RET_CHUNK, LANE)), per_head((RET_CHUNK, LANE)),
                  per_head((1, LANE)), per_head((1, RET_D)), per_head((1, RET_D))],
        out_specs=[pl.BlockSpec((RET_CHUNK, RET_D), lambda b, h, c: (b * nc + c, h)),
                   pl.BlockSpec((None, None, RET_D, RET_D), lambda b, h, c: (b, h, 0, 0))],
        out_shape=[jax.ShapeDtypeStruct((batch * seq, RET_HEADS * RET_D), F32),
                   jax.ShapeDtypeStruct((batch, RET_HEADS, RET_D, RET_D), F32)],
        scratch_shapes=[pltpu.VMEM((RET_D, RET_D), F32)],
        compiler_params=_cp("parallel", "parallel", "arbitrary"),
        name="retention_prompt",
    )(proj, proj, proj, proj, dec, qd, kd, cd, gn_g[:, None, :], gn_b[:, None, :])


def _ret_sample_kernel(q_ref, k_ref, v_ref, g_ref, s0_ref, dec_ref, qd_ref, kd_ref, cd_ref, gg_ref, gb_ref,
                       o_ref, s_out_ref, *, ts):
    pad = jnp.zeros((RET_CHUNK - ts, RET_D), F32)

    def padded(ref, h):
        return jnp.concatenate([ref[:, h * RET_D:(h + 1) * RET_D], pad], axis=0)

    for h in range(RET_HEADS):
        out, s_new = _ret_core(padded(q_ref, h), padded(k_ref, h), padded(v_ref, h), padded(g_ref, h),
                               s0_ref[h], dec_ref[h], qd_ref[h], kd_ref[h], cd_ref[h], gg_ref[h], gb_ref[h])
        o_ref[:, h * RET_D:(h + 1) * RET_D] = out[:ts]
        s_out_ref[h] = s_new


def _ret_sample(proj, state, tabs, gn_g, gn_b, mp, bd, ts):
    width = RET_HEADS * RET_D
    row0 = mp // ts

    def tok(j):
        return pl.BlockSpec((ts, width), lambda b: (row0 + b, j))

    def const(a):
        return pl.BlockSpec(a.shape, lambda b: (0,) * a.ndim)

    dec, qd, kd, cd = tabs
    gg, gb = gn_g[:, None, :], gn_b[:, None, :]
    return pl.pallas_call(
        functools.partial(_ret_sample_kernel, ts=ts),
        grid=(bd,),
        in_specs=[tok(0), tok(1), tok(2), tok(3),
                  pl.BlockSpec((None, RET_HEADS, RET_D, RET_D), lambda b: (b, 0, 0, 0)),
                  const(dec), const(qd), const(kd), const(cd), const(gg), const(gb)],
        out_specs=[pl.BlockSpec((ts, width), lambda b: (b, 0)),
                   pl.BlockSpec((None, RET_HEADS, RET_D, RET_D), lambda b: (b, 0, 0, 0))],
        out_shape=[jax.ShapeDtypeStruct((bd * ts, width), F32),
                   jax.ShapeDtypeStruct(state.shape, F32)],
        compiler_params=_cp("parallel"),
        name="retention_sample",
    )(proj, proj, proj, proj, state, dec, qd, kd, cd, gg, gb)


def _ret_tables(chunk_len):
    h = jnp.arange(RET_HEADS, dtype=F32)
    lg = jnp.log(1.0 - jnp.exp2(-5.0 - h))
    idx = jnp.arange(RET_CHUNK, dtype=F32)
    diff = idx[:, None] - idx[None, :]
    dec = jnp.where(diff >= 0, jnp.exp(jnp.maximum(diff, 0.0)[None] * lg[:, None, None]), 0.0)
    qd = jnp.exp((idx + 1.0)[None, :] * lg[:, None])
    kd = jnp.exp((chunk_len - 1.0 - idx)[None, :] * lg[:, None])
    kd = jnp.where(idx[None, :] < chunk_len, kd, 0.0)
    cd = jnp.exp(chunk_len * lg)
    rep = lambda a: jnp.broadcast_to(a[..., None], a.shape + (LANE,))
    return dec, rep(qd), rep(kd), rep(cd[:, None])


def _mla_prompt_kernel(q_ref, k_ref, v_ref, o_ref, *, tq):
    i = pl.program_id(2)
    q = q_ref[...]

    def step(j, carry, masked):
        m, l, acc = carry
        rows = pl.ds(pl.multiple_of(j * tq, tq), tq)
        s = _dot_nt(q, k_ref[rows, :]) * MLA_SCALE
        if masked:
            r = lax.broadcasted_iota(jnp.int32, (tq, tq), 0)
            c = lax.broadcasted_iota(jnp.int32, (tq, tq), 1)
            s = jnp.where(c <= r, s, NEG)
        m_new = jnp.maximum(m, jnp.max(s, axis=1, keepdims=True))
        a = jnp.exp(m - m_new)
        p = jnp.exp(s - m_new)
        l = a * l + jnp.sum(p, axis=1, keepdims=True)
        acc = a * acc + _dot(p.astype(BF16), v_ref[rows, :])
        return m_new, l, acc

    init = (jnp.full((tq, 1), NEG, F32), jnp.zeros((tq, 1), F32), jnp.zeros((tq, MLA_V), F32))
    carry = lax.fori_loop(0, i, lambda j, c: step(j, c, False), init)
    m, l, acc = step(i, carry, True)
    o_ref[...] = acc / l


def _mla_prompt(q_cat, k_cat, v, batch, seq):
    tq = _pick(seq, (512, 256, 128))
    nq = seq // tq
    return pl.pallas_call(
        functools.partial(_mla_prompt_kernel, tq=tq),
        grid=(batch, MLA_HEADS, nq),
        in_specs=[pl.BlockSpec((tq, MLA_QW), lambda b, h, i: (b * nq + i, h)),
                  pl.BlockSpec((seq, MLA_QW), lambda b, h, i: (b, h)),
                  pl.BlockSpec((seq, MLA_V), lambda b, h, i: (b, h))],
        out_specs=pl.BlockSpec((tq, MLA_V), lambda b, h, i: (b * nq + i, h)),
        out_shape=jax.ShapeDtypeStruct((batch * seq, MLA_HEADS * MLA_V), F32),
        compiler_params=_cp("parallel", "parallel", "arbitrary"),
        name="mla_prompt_attention",
    )(q_cat, k_cat, v)


def _qlat_kernel(q_ref, w_ref, o_ref):
    o_ref[...] = _dot(q_ref[...], w_ref[...])


def _q_latent(q_cat, w_uk_t, mp, ms):
    tm = _pick(ms, ROW_TILES[:-1])
    row0 = mp // tm
    return pl.pallas_call(
        _qlat_kernel,
        grid=(MLA_HEADS, ms // tm),
        in_specs=[pl.BlockSpec((tm, MLA_NOPE), lambda h, i: (row0 + i, 2 * h)),
                  pl.BlockSpec((None, MLA_NOPE, MLA_RANK), lambda h, i: (h, 0, 0))],
        out_specs=pl.BlockSpec((None, tm, MLA_RANK), lambda h, i: (h, i, 0)),
        out_shape=jax.ShapeDtypeStruct((MLA_HEADS, ms, MLA_RANK), F32),
        compiler_params=_cp("parallel", "parallel"),
        name="mla_q_latent",
    )(q_cat, w_uk_t)


def _mla_sample_kernel(pt_ref, ql_ref, qp_ref, ckvn_ref, kpen_ref, *refs, npg, ts):
    ckv_refs = refs[:npg]
    kpe_refs = refs[npg:2 * npg]
    o_ref, m_ref, l_ref, acc_ref, kv_ref, kp_ref = refs[2 * npg:]
    c = pl.program_id(1)
    nc = pl.num_programs(1)
    rows = MLA_HEADS * ts

    @pl.when(c == 0)
    def _():
        m_ref[...] = jnp.full_like(m_ref, NEG)
        l_ref[...] = jnp.zeros_like(l_ref)
        acc_ref[...] = jnp.zeros_like(acc_ref)

    ql = ql_ref[...].reshape(rows, MLA_RANK).astype(BF16)
    qp = qp_ref[...].astype(BF16)

    def update(s, v):
        m = m_ref[...]
        m_new = jnp.maximum(m, jnp.max(s, axis=1, keepdims=True))
        a = jnp.exp(m - m_new)
        p = jnp.exp(s - m_new)
        l_ref[...] = a * l_ref[...] + jnp.sum(p, axis=1, keepdims=True)
        acc_ref[...] = a * acc_ref[...] + _dot(p.astype(BF16), v)
        m_ref[...] = m_new

    for j in range(npg):
        kv_ref[j * PAGE:(j + 1) * PAGE, :] = ckv_refs[j][...].astype(BF16)
        kp_ref[:, j * PAGE:(j + 1) * PAGE] = kpe_refs[j][...].astype(BF16)
    kv = kv_ref[...]
    update((_dot_nt(ql, kv) + _dot(qp, kp_ref[...])) * MLA_SCALE, kv)

    @pl.when(c == nc - 1)
    def _():
        ckv = jnp.concatenate([ckvn_ref[...], jnp.zeros((PAGE - ts, MLA_RANK), F32)], axis=0).astype(BF16)
        kpe = jnp.concatenate([kpen_ref[...], jnp.zeros((PAGE - ts, MLA_ROPE), F32)], axis=0).astype(BF16)
        s = (_dot_nt(ql, ckv) + _dot_nt(qp, kpe)) * MLA_SCALE
        t = lax.broadcasted_iota(jnp.int32, (rows, PAGE), 0) % ts
        key = lax.broadcasted_iota(jnp.int32, (rows, PAGE), 1)
        update(jnp.where(key <= t, s, NEG), ckv)
        o_ref[...] = acc_ref[...] / l_ref[...]


def _mla_sample(page_table_flat, q_lat, q_pe, ckv_new, kpe_new, cache_ckv, cache_kpe_t, layer, bd, ts, n_pages):
    npg = _pick(n_pages, (8, 4, 2, 1))
    nchunk = n_pages // npg
    rows = MLA_HEADS * ts

    def page_spec(shape, j):
        return pl.BlockSpec((None, None) + shape,
                            lambda b, c, pt: (layer, pt[b * n_pages + c * npg + j], 0, 0))

    in_specs = [pl.BlockSpec((MLA_HEADS, ts, MLA_RANK), lambda b, c, pt: (0, b, 0)),
                pl.BlockSpec((None, rows, MLA_ROPE), lambda b, c, pt: (b, 0, 0)),
                pl.BlockSpec((None, ts, MLA_RANK), lambda b, c, pt: (b, 0, 0)),
                pl.BlockSpec((None, ts, MLA_ROPE), lambda b, c, pt: (b, 0, 0))]
    in_specs += [page_spec((PAGE, MLA_RANK), j) for j in range(npg)]
    in_specs += [page_spec((MLA_ROPE, PAGE), j) for j in range(npg)]
    return pl.pallas_call(
        functools.partial(_mla_sample_kernel, npg=npg, ts=ts),
        grid_spec=pltpu.PrefetchScalarGridSpec(
            num_scalar_prefetch=1,
            grid=(bd, nchunk),
            in_specs=in_specs,
            out_specs=pl.BlockSpec((None, rows, MLA_RANK), lambda b, c, pt: (b, 0, 0)),
            scratch_shapes=[pltpu.VMEM((rows, 1), F32), pltpu.VMEM((rows, 1), F32),
                            pltpu.VMEM((rows, MLA_RANK), F32),
                            pltpu.VMEM((npg * PAGE, MLA_RANK), BF16), pltpu.VMEM((MLA_ROPE, npg * PAGE), BF16)],
        ),
        out_shape=jax.ShapeDtypeStruct((bd, rows, MLA_RANK), F32),
        compiler_params=_cp("parallel", "arbitrary"),
        name="mla_sample_attention",
    )(page_table_flat, q_lat, q_pe, ckv_new, kpe_new, *([cache_ckv] * npg), *([cache_kpe_t] * npg))


def _uv_kernel(o_ref, w_ref, out_ref, *, ts):
    tb = o_ref.shape[0]
    o = o_ref[...].reshape(tb * ts, MLA_RANK).astype(BF16)
    out_ref[...] = _dot(o, w_ref[...])


def _mla_value_up(o_lat, w_uv, bd, ts):
    tb = _pick(bd, (64, 32, 16, 8, 4, 2, 1))
    return pl.pallas_call(
        functools.partial(_uv_kernel, ts=ts),
        grid=(MLA_HEADS, bd // tb),
        in_specs=[pl.BlockSpec((tb, None, ts, MLA_RANK), lambda h, i: (i, h, 0, 0)),
                  pl.BlockSpec((None, MLA_RANK, MLA_V), lambda h, i: (h, 0, 0))],
        out_specs=pl.BlockSpec((tb * ts, MLA_V), lambda h, i: (i, h)),
        out_shape=jax.ShapeDtypeStruct((bd * ts, MLA_HEADS * MLA_V), F32),
        compiler_params=_cp("parallel", "parallel"),
        name="mla_value_up_proj",
    )(o_lat, w_uv)


def _mm_kernel(x_ref, w_ref, o_ref, xb_ref):
    @pl.when(pl.program_id(1) == 0)
    def _():
        xb_ref[...] = x_ref[...].astype(BF16)

    o_ref[...] = _dot(xb_ref[...], w_ref[...])


def _matmul(x, w):
    m, d = x.shape
    n = w.shape[1]
    tm = _pick(m, ROW_TILES)
    tn = _pick(n, (512, 256, 128))
    return pl.pallas_call(
        _mm_kernel,
        grid=(m // tm, n // tn),
        in_specs=[pl.BlockSpec((tm, d), lambda i, j: (i, 0)),
                  pl.BlockSpec((d, tn), lambda i, j: (0, j))],
        out_specs=pl.BlockSpec((tm, tn), lambda i, j: (i, j)),
        out_shape=jax.ShapeDtypeStruct((m, n), F32),
        scratch_shapes=[pltpu.VMEM((tm, d), BF16)],
        compiler_params=_cp("parallel", "arbitrary"),
        name="sb_in_proj",
    )(x, w)


SB_Q_COLS = SB_HEADS * SB_DIM
SB_KV_COLS = SB_KV_HEADS * SB_DIM
SB_TK = 128
SB_EXIT = -104.0


def _sb_block(z, mask, carry, tri_ones):
    r = z.shape[0]
    sp = jnp.maximum(z, 0.0) + jnp.log1p(jnp.exp(-jnp.abs(z)))
    if mask is not None:
        sp_m = jnp.where(mask, sp, 0.0)
    else:
        sp_m = sp
    hi = sp_m.astype(BF16)
    lo = (sp_m - hi.astype(F32)).astype(BF16)
    cs = _dot(jnp.concatenate([hi, lo], axis=0), tri_ones)
    cs = cs[:r] + cs[r:]
    a = jnp.exp(z - sp + carry - cs[:, :SB_TK])
    if mask is not None:
        a = jnp.where(mask, a, 0.0)
    return a, carry - cs[:, SB_TK:]


def _sb_live(carry):
    return (jnp.max(carry) > SB_EXIT).astype(jnp.int32)


def _sb_prompt_kernel(q_ref, k_ref, v_ref, to_ref, o_ref, *, tq):
    i = pl.program_id(2)
    rows = SB_GROUP * tq
    q = jnp.concatenate([q_ref[:, g * SB_DIM:(g + 1) * SB_DIM] for g in range(SB_GROUP)], axis=0).astype(BF16)
    tri_ones = to_ref[...]

    def step(j, carry, acc, mask):
        ks = pl.ds(pl.multiple_of(j * SB_TK, SB_TK), SB_TK)
        z = _dot_nt(q, k_ref[ks, :].astype(BF16)) * SB_SCALE
        a, carry = _sb_block(z, mask, carry, tri_ones)
        return carry, acc + _dot(a.astype(BF16), v_ref[ks, :].astype(BF16))

    t = lax.broadcasted_iota(jnp.int32, (rows, SB_TK), 0) % tq
    key = lax.broadcasted_iota(jnp.int32, (rows, SB_TK), 1)
    carry, acc = step(i, jnp.zeros((rows, SB_TK), F32), jnp.zeros((rows, SB_DIM), F32), key < t)

    def body(state):
        j, _, carry, acc = state
        carry, acc = step(j, carry, acc, None)
        return j - 1, _sb_live(carry), carry, acc

    _, _, carry, acc = lax.while_loop(lambda s: (s[0] >= 0) & (s[1] > 0), body,
                                      (i - 1, _sb_live(carry), carry, acc))
    for g in range(SB_GROUP):
        o_ref[:, g * SB_DIM:(g + 1) * SB_DIM] = acc[g * tq:(g + 1) * tq]


def _sb_prompt(proj, tri_ones, batch, seq):
    tq = SB_TK
    nq = seq // tq
    gw = SB_GROUP * SB_DIM
    kb0 = SB_Q_COLS // SB_DIM
    vb0 = kb0 + SB_KV_HEADS
    return pl.pallas_call(
        functools.partial(_sb_prompt_kernel, tq=tq),
        grid=(batch, SB_KV_HEADS, nq),
        in_specs=[pl.BlockSpec((tq, gw), lambda b, h, i: (b * nq + i, h)),
                  pl.BlockSpec((seq, SB_DIM), lambda b, h, i: (b, kb0 + h)),
                  pl.BlockSpec((seq, SB_DIM), lambda b, h, i: (b, vb0 + h)),
                  pl.BlockSpec((SB_TK, 2 * SB_TK), lambda b, h, i: (0, 0))],
        out_specs=pl.BlockSpec((tq, gw), lambda b, h, i: (b * nq + i, h)),
        out_shape=jax.ShapeDtypeStruct((batch * seq, SB_Q_COLS), F32),
        compiler_params=_cp("parallel", "parallel", "arbitrary"),
        name="sb_prompt_attention",
    )(proj, proj, proj, tri_ones)


def _sb_sample_kernel(pt_ref, q_ref, kn_ref, vn_ref, to_ref, kc_ref, vc_ref, o_ref, kbuf, vbuf, sem,
                      *, ts, n_pages):
    b = pl.program_id(0)
    hrows = SB_GROUP * ts
    rows = SB_KV_HEADS * hrows
    q = jnp.concatenate([q_ref[:, h * SB_DIM:(h + 1) * SB_DIM] for h in range(SB_HEADS)], axis=0).astype(BF16)
    tri_ones = to_ref[...]

    def page_copies(p, slot):
        page = pt_ref[b * n_pages + p]
        return (pltpu.make_async_copy(kc_ref.at[page], kbuf.at[slot], sem.at[0, slot]),
                pltpu.make_async_copy(vc_ref.at[page], vbuf.at[slot], sem.at[1, slot]))

    def slot_of(p):
        return lax.rem(n_pages - 1 - p, 2)

    def block(k_of, v_of, mask, carry, acc):
        z = jnp.concatenate(
            [_dot_nt(q[h * hrows:(h + 1) * hrows], k_of(h).astype(BF16)) for h in range(SB_KV_HEADS)],
            axis=0) * SB_SCALE
        a, carry = _sb_block(z, mask, carry, tri_ones)
        ab = a.astype(BF16)
        upd = jnp.concatenate(
            [_dot(ab[h * hrows:(h + 1) * hrows], v_of(h).astype(BF16)) for h in range(SB_KV_HEADS)],
            axis=0)
        return carry, acc + upd

    for cp in page_copies(n_pages - 1, 0):
        cp.start()

    zpad = jnp.zeros((PAGE - ts, SB_KV_COLS), F32)
    kn = jnp.concatenate([kn_ref[...], zpad], axis=0)
    vn = jnp.concatenate([vn_ref[...], zpad], axis=0)
    t = lax.broadcasted_iota(jnp.int32, (rows, SB_TK), 0) % ts
    key = lax.broadcasted_iota(jnp.int32, (rows, SB_TK), 1)
    carry, acc = block(lambda h: kn[:, h * SB_DIM:(h + 1) * SB_DIM], lambda h: vn[:, h * SB_DIM:(h + 1) * SB_DIM],
                       key < t, jnp.zeros((rows, SB_TK), F32), jnp.zeros((rows, SB_DIM), F32))

    def body(state):
        p, _, carry, acc = state
        slot = slot_of(p)
        for cp in page_copies(p, slot):
            cp.wait()

        @pl.when(p > 0)
        def _():
            for cp in page_copies(p - 1, 1 - slot):
                cp.start()

        def head_rows(buf, h):
            return buf[slot, pl.ds(h, PAGE, stride=SB_KV_HEADS), :]

        carry, acc = block(lambda h: head_rows(kbuf, h), lambda h: head_rows(vbuf, h), None, carry, acc)
        return p - 1, _sb_live(carry), carry, acc

    p, _, carry, acc = lax.while_loop(lambda s: (s[0] >= 0) & (s[1] > 0), body,
                                      (jnp.int32(n_pages - 1), _sb_live(carry), carry, acc))

    @pl.when(p >= 0)
    def _():
        for cp in page_copies(p, slot_of(p)):
            cp.wait()

    for h in range(SB_HEADS):
        o_ref[:, h * SB_DIM:(h + 1) * SB_DIM] = acc[h * ts:(h + 1) * ts]


def _sb_sample(page_table_flat, proj, tri_ones, cache_k, cache_v, mp, bd, ts, n_pages):
    row0 = mp // ts
    kb0 = SB_Q_COLS // SB_KV_COLS
    page_rows = PAGE * SB_KV_HEADS
    in_specs = [pl.BlockSpec((ts, SB_Q_COLS), lambda b, pt: (row0 + b, 0)),
                pl.BlockSpec((ts, SB_KV_COLS), lambda b, pt: (row0 + b, kb0)),
                pl.BlockSpec((ts, SB_KV_COLS), lambda b, pt: (row0 + b, kb0 + 1)),
                pl.BlockSpec((SB_TK, 2 * SB_TK), lambda b, pt: (0, 0)),
                pl.BlockSpec(memory_space=pl.ANY),
                pl.BlockSpec(memory_space=pl.ANY)]
    return pl.pallas_call(
        functools.partial(_sb_sample_kernel, ts=ts, n_pages=n_pages),
        grid_spec=pltpu.PrefetchScalarGridSpec(
            num_scalar_prefetch=1,
            grid=(bd,),
            in_specs=in_specs,
            out_specs=pl.BlockSpec((ts, SB_Q_COLS), lambda b, pt: (b, 0)),
            scratch_shapes=[pltpu.VMEM((2, page_rows, SB_DIM), F32), pltpu.VMEM((2, page_rows, SB_DIM), F32),
                            pltpu.SemaphoreType.DMA((2, 2))],
        ),
        out_shape=jax.ShapeDtypeStruct((bd * ts, SB_Q_COLS), F32),
        compiler_params=_cp("arbitrary"),
        name="sb_sample_attention",
    )(page_table_flat, proj, proj, proj, tri_ones, cache_k, cache_v)


def _rope_tables(pos):
    p = pos.astype(F32)[:, None]
    inv_r = 1.0 / (ROPE_BASE ** jnp.linspace(0.0, 1.0, RET_D // 2, dtype=F32))
    ang = p * inv_r[None, :]
    c, s = jnp.cos(ang), jnp.sin(ang)
    cr = jnp.concatenate([c, c], axis=1)
    sr = jnp.concatenate([-s, s], axis=1)
    inv_m = 1.0 / (ROPE_BASE ** (jnp.arange(0, MLA_ROPE, 2, dtype=F32) / MLA_ROPE))
    ang = p * inv_m[None, :]
    c, s = jnp.cos(ang), jnp.sin(ang)
    z = jnp.zeros((pos.shape[0], LANE - MLA_ROPE), F32)
    t1 = jnp.concatenate([c, c, z], axis=1)
    t2 = jnp.concatenate([-s, s, z], axis=1)
    return cr, sr, t1, t2


def _swap_halves(w):
    half = w.shape[-1] // 2
    return jnp.concatenate([w[..., half:], w[..., :half]], axis=-1)


def kernel(x_prompt, x_sample, state_ret, cache_mla_ckv, cache_mla_kpe, cache_sb_k, cache_sb_v, page_table,
           ln_g, ln_b, ffn_w_gate, ffn_w_up, ffn_w_down,
           ev_w_in, ev_q_norm, ev_w_uq, ev_kv_norm, ev_w_ukv, ev_gn_g, ev_gn_b, ev_w_o,
           od_w_in, od_w_o):
    batch, seq, d = x_prompt.shape
    bd, ts, _ = x_sample.shape
    n_pages = page_table.shape[1]
    past = n_pages * PAGE
    mp, ms = batch * seq, bd * ts
    assert seq % RET_CHUNK == 0 and ts <= 8 and mp % ts == 0

    x = jnp.concatenate([x_prompt.reshape(mp, d), x_sample.reshape(ms, d)], axis=0)
    pos = jnp.concatenate([jnp.tile(jnp.arange(seq), batch), jnp.tile(past + jnp.arange(ts), bd)])
    cr, sr, t1, t2 = _rope_tables(pos)
    pt_flat = page_table.reshape(-1)
    tri = (jnp.arange(SB_TK)[:, None] > jnp.arange(SB_TK)[None, :]).astype(BF16)
    tri_ones = jnp.concatenate([tri, jnp.ones((SB_TK, SB_TK), BF16)], axis=1)
    tabs_p = _ret_tables(float(RET_CHUNK))
    tabs_s = _ret_tables(float(ts))

    wg, wu, wd = ffn_w_gate.astype(BF16), ffn_w_up.astype(BF16), ffn_w_down.astype(BF16)

    outs = {k: [] for k in ("ret_p", "ret_s", "ckv_p", "ckv_s", "kpe_p", "kpe_s", "sbk_p", "sbk_s", "sbv_p", "sbv_s")}
    for l in range(DEPTH):
        x = _ffn_half_step(x, wg, wu, wd, l, 0, ln_g[l, 0], ln_b[l, 0])
        if l % 2 == 0:
            e = l // 2
            w_in = ev_w_in[e]
            w_main = w_in[:, :EV_MAIN_COLS].astype(BF16)
            w_kpe = w_in[:, EV_MAIN_COLS:]
            w_kpe = jnp.concatenate([w_kpe, _swap_halves(w_kpe)], axis=1).astype(BF16)
            proj = _ev_proj(x, w_main, cr, sr, ev_q_norm[e], ev_kv_norm[e])
            kpe_pad = _kpe_proj(x, w_kpe, t1, t2)

            w_uq = ev_w_uq[e].reshape(MLA_RANK, MLA_HEADS, MLA_NOPE + MLA_ROPE)
            w_uq_ext = jnp.concatenate([w_uq, _swap_halves(w_uq[..., MLA_NOPE:])], axis=-1)
            q_cat = _uq_proj(proj, w_uq_ext.reshape(MLA_RANK, MLA_HEADS * MLA_QW).astype(BF16), t1, t2)

            w_ukv = ev_w_ukv[e]
            k_cat, v_p = _ukv_proj(proj, w_ukv.astype(BF16), kpe_pad, mp)

            ret_out_p, s_p = _ret_prompt(proj, tabs_p, ev_gn_g[e], ev_gn_b[e], batch, seq)
            ret_out_s, s_s = _ret_sample(proj, state_ret[e], tabs_s, ev_gn_g[e], ev_gn_b[e], mp, bd, ts)
            mla_out_p = _mla_prompt(q_cat, k_cat, v_p, batch, seq)

            w3 = w_ukv.reshape(MLA_RANK, MLA_HEADS, MLA_NOPE + MLA_V)
            w_uk_t = w3[..., :MLA_NOPE].transpose(1, 2, 0).astype(BF16)
            w_uv = w3[..., MLA_NOPE:].transpose(1, 0, 2).astype(BF16)
            q_lat = _q_latent(q_cat, w_uk_t, mp, ms)
            q_pe_s = q_cat[mp:].reshape(bd, ts, MLA_HEADS, MLA_QW)[..., MLA_NOPE:MLA_NOPE + MLA_ROPE]
            q_pe_s = q_pe_s.transpose(0, 2, 1, 3).reshape(bd, MLA_HEADS * ts, MLA_ROPE)
            ckv = proj[:, EV_CKV_BLOCK * EV_BLOCK:(EV_CKV_BLOCK + 1) * EV_BLOCK]
            kpe = kpe_pad[:, :MLA_ROPE]
            ckv_s = ckv[mp:].reshape(bd, ts, MLA_RANK)
            kpe_s = kpe[mp:].reshape(bd, ts, MLA_ROPE)
            o_lat = _mla_sample(pt_flat, q_lat, q_pe_s, ckv_s, kpe_s, cache_mla_ckv,
                                jnp.swapaxes(cache_mla_kpe, 2, 3), e, bd, ts, n_pages)
            mla_out_s = _mla_value_up(o_lat.reshape(bd, MLA_HEADS, ts, MLA_RANK), w_uv, bd, ts)

            w_o = ev_w_o[e].astype(BF16)
            n_ret = RET_HEADS * RET_D
            x = _out_proj_ln([jnp.concatenate([ret_out_p, ret_out_s], axis=0),
                              jnp.concatenate([mla_out_p, mla_out_s], axis=0)],
                             [w_o[:n_ret], w_o[n_ret:]], x, ln_g[l, 1], ln_b[l, 1])
            outs["ret_p"].append(s_p)
            outs["ret_s"].append(s_s)
            outs["ckv_p"].append(ckv[:mp].reshape(batch, seq, MLA_RANK))
            outs["ckv_s"].append(ckv_s)
            outs["kpe_p"].append(kpe[:mp].reshape(batch, seq, MLA_ROPE))
            outs["kpe_s"].append(kpe_s)
        else:
            o = l // 2
            proj = _matmul(x, od_w_in[o].astype(BF16))
            n_kv = cache_sb_k.shape[1]
            cache_k = cache_sb_k[o].reshape(n_kv, PAGE * SB_KV_HEADS, SB_DIM)
            cache_v = cache_sb_v[o].reshape(n_kv, PAGE * SB_KV_HEADS, SB_DIM)
            att_p = _sb_prompt(proj, tri_ones, batch, seq)
            att_s = _sb_sample(pt_flat, proj, tri_ones, cache_k, cache_v, mp, bd, ts, n_pages)
            x = _out_proj_ln([jnp.concatenate([att_p, att_s], axis=0)], [od_w_o[o].astype(BF16)],
                             x, ln_g[l, 1], ln_b[l, 1])
            k = proj[:, SB_Q_COLS:SB_Q_COLS + SB_KV_COLS]
            v = proj[:, SB_Q_COLS + SB_KV_COLS:]
            outs["sbk_p"].append(k[:mp].reshape(batch, seq, SB_KV_HEADS, SB_DIM))
            outs["sbk_s"].append(k[mp:].reshape(bd, ts, SB_KV_HEADS, SB_DIM))
            outs["sbv_p"].append(v[:mp].reshape(batch, seq, SB_KV_HEADS, SB_DIM))
            outs["sbv_s"].append(v[mp:].reshape(bd, ts, SB_KV_HEADS, SB_DIM))
        x = _ffn_half_step(x, wg, wu, wd, l, 1, ln_g[l, 2], ln_b[l, 2])

    y_p = x[:mp].reshape(batch, seq, d)
    y_s = x[mp:].reshape(bd, ts, d)
    return (y_p, y_s, jnp.stack(outs["ret_p"]), jnp.stack(outs["ret_s"]),
            jnp.stack(outs["ckv_p"]), jnp.stack(outs["ckv_s"]), jnp.stack(outs["kpe_p"]), jnp.stack(outs["kpe_s"]),
            jnp.stack(outs["sbk_p"]), jnp.stack(outs["sbk_s"]), jnp.stack(outs["sbv_p"]), jnp.stack(outs["sbv_s"]))
```

```python
import functools

import jax
import jax.numpy as jnp
import numpy as np
from jax import lax
from jax.experimental import pallas as pl
from jax.experimental.pallas import tpu as pltpu

F32 = jnp.float32
BF16 = jnp.bfloat16

DEPTH = 2
PAGE = 128
RET_HEADS = 8
RET_D = 128
MLA_HEADS = 8
MLA_NOPE = 128
MLA_ROPE = 64
MLA_V = 128
MLA_RANK = 512
MLA_SCALE = (MLA_NOPE + MLA_ROPE) ** -0.5
SB_HEADS = 16
SB_KV_HEADS = 4
SB_GROUP = SB_HEADS // SB_KV_HEADS
SB_DIM = 128
SB_SCALE = SB_DIM ** -0.5
ROPE_BASE = 10000.0
EPS = 1e-5
ALPHA = (2 * DEPTH) ** 0.25
RET_CHUNK = 128
NEG = -1e30

LANE = 128
VMEM_LIMIT = 56 * 1024 * 1024
ROW_TILES = (512, 256, 128, 64, 32, 16, 8)
PROJ_ROW_TILES = (1024,) + ROW_TILES


def _pick(n, cands):
    for c in cands:
        if n % c == 0:
            return c
    raise ValueError(f"no tile for {n}")


def _cp(*sem):
    return pltpu.CompilerParams(dimension_semantics=sem, vmem_limit_bytes=VMEM_LIMIT)


def _dot(a, b):
    return jnp.dot(a, b, preferred_element_type=F32)


def _dot_nt(a, b):
    return lax.dot_general(a, b, (((1,), (1,)), ((), ())), preferred_element_type=F32)


def _layer_norm(y, g, b):
    mu = jnp.mean(y, axis=-1, keepdims=True)
    d = y - mu
    var = jnp.mean(d * d, axis=-1, keepdims=True)
    return d * lax.rsqrt(var + EPS) * g + b


def _rms_norm(y, g):
    return y * lax.rsqrt(jnp.mean(y * y, axis=-1, keepdims=True) + EPS) * g


def _half_swap(x):
    return pltpu.roll(x, LANE // 2, 1)


def _ffn_kernel(x_ref, wg_ref, wu_ref, wd_ref, g_ref, b_ref, o_ref, xb_ref, acc_ref, *, nf):
    f = pl.program_id(1)

    @pl.when(f == 0)
    def _():
        xb_ref[...] = x_ref[...].astype(BF16)
        acc_ref[...] = jnp.zeros_like(acc_ref)

    xb = xb_ref[...]
    g = _dot(xb, wg_ref[...])
    u = _dot(xb, wu_ref[...])
    a = (g * jax.nn.sigmoid(g) * u).astype(BF16)
    acc_ref[...] += _dot(a, wd_ref[...])

    @pl.when(f == nf - 1)
    def _():
        y = ALPHA * x_ref[...] + 0.5 * acc_ref[...]
        o_ref[...] = _layer_norm(y, g_ref[...], b_ref[...])


def _ffn_half_step(x, wg, wu, wd, layer, half, g, b):
    m, d = x.shape
    ff = wg.shape[-1]
    tm = _pick(m, ROW_TILES)
    tf = _pick(ff, (512, 256, 128))
    nf = ff // tf
    return pl.pallas_call(
        functools.partial(_ffn_kernel, nf=nf),
        grid=(m // tm, nf),
        in_specs=[
            pl.BlockSpec((tm, d), lambda i, f: (i, 0)),
            pl.BlockSpec((None, None, d, tf), lambda i, f: (layer, half, 0, f)),
            pl.BlockSpec((None, None, d, tf), lambda i, f: (layer, half, 0, f)),
            pl.BlockSpec((None, None, tf, d), lambda i, f: (layer, half, f, 0)),
            pl.BlockSpec((1, d), lambda i, f: (0, 0)),
            pl.BlockSpec((1, d), lambda i, f: (0, 0)),
        ],
        out_specs=pl.BlockSpec((tm, d), lambda i, f: (i, 0)),
        out_shape=jax.ShapeDtypeStruct((m, d), F32),
        scratch_shapes=[pltpu.VMEM((tm, d), BF16), pltpu.VMEM((tm, d), F32)],
        compiler_params=_cp("parallel", "arbitrary"),
        name="ffn_half_step",
    )(x, wg, wu, wd, g.reshape(1, d), b.reshape(1, d))


def _out_ln_kernel(*refs, n_parts):
    a_refs = refs[:n_parts]
    w_refs = refs[n_parts:2 * n_parts]
    x_ref, g_ref, b_ref, o_ref = refs[2 * n_parts:]
    acc = _dot(a_refs[0][...].astype(BF16), w_refs[0][...])
    for a_ref, w_ref in zip(a_refs[1:], w_refs[1:]):
        acc += _dot(a_ref[...].astype(BF16), w_ref[...])
    y = ALPHA * x_ref[...] + acc
    o_ref[...] = _layer_norm(y, g_ref[...], b_ref[...])


def _out_proj_ln(parts, weights, x, g, b):
    m, d = x.shape
    tm = _pick(m, ROW_TILES)
    n = len(parts)
    in_specs = [pl.BlockSpec((tm, p.shape[1]), lambda i: (i, 0)) for p in parts]
    in_specs += [pl.BlockSpec(w.shape, lambda i: (0, 0)) for w in weights]
    in_specs += [pl.BlockSpec((tm, d), lambda i: (i, 0)),
                 pl.BlockSpec((1, d), lambda i: (0, 0)),
                 pl.BlockSpec((1, d), lambda i: (0, 0))]
    return pl.pallas_call(
        functools.partial(_out_ln_kernel, n_parts=n),
        grid=(m // tm,),
        in_specs=in_specs,
        out_specs=pl.BlockSpec((tm, d), lambda i: (i, 0)),
        out_shape=jax.ShapeDtypeStruct((m, d), F32),
        compiler_params=_cp("parallel"),
        name="out_proj_ln",
    )(*parts, *weights, x, g.reshape(1, d), b.reshape(1, d))


EV_BLOCK = 512
EV_Q_BLOCKS = (0, 2)
EV_K_BLOCKS = (2, 4)
EV_CQ_BLOCK = 8
EV_CKV_BLOCK = 9
EV_MAIN_COLS = 10 * EV_BLOCK


def _ev_proj_kernel(x_ref, w_ref, cr_ref, sr_ref, gq_ref, gkv_ref, o_ref, xb_ref):
    j = pl.program_id(1)

    @pl.when(j == 0)
    def _():
        xb_ref[...] = x_ref[...].astype(BF16)

    acc = _dot(xb_ref[...], w_ref[...])

    def rope(scale):
        c, s = cr_ref[...], sr_ref[...]
        for h in range(EV_BLOCK // LANE):
            blk = acc[:, h * LANE:(h + 1) * LANE]
            r = blk * c + _half_swap(blk) * s
            o_ref[:, h * LANE:(h + 1) * LANE] = r if scale is None else r * scale

    @pl.when(j < EV_Q_BLOCKS[1])
    def _():
        rope(None)

    @pl.when((j >= EV_K_BLOCKS[0]) & (j < EV_K_BLOCKS[1]))
    def _():
        rope(RET_D ** -0.5)

    @pl.when((j >= EV_K_BLOCKS[1]) & (j < EV_CQ_BLOCK))
    def _():
        o_ref[...] = acc

    @pl.when(j == EV_CQ_BLOCK)
    def _():
        o_ref[...] = _rms_norm(acc, gq_ref[...])

    @pl.when(j == EV_CKV_BLOCK)
    def _():
        o_ref[...] = _rms_norm(acc, gkv_ref[...])


def _ev_proj(x, w_main, cr, sr, gq, gkv):
    m, d = x.shape
    tm = _pick(m, PROJ_ROW_TILES)
    nb = EV_MAIN_COLS // EV_BLOCK
    return pl.pallas_call(
        _ev_proj_kernel,
        grid=(m // tm, nb),
        in_specs=[
            pl.BlockSpec((tm, d), lambda i, j: (i, 0)),
            pl.BlockSpec((d, EV_BLOCK), lambda i, j: (0, j)),
            pl.BlockSpec((tm, LANE), lambda i, j: (i, 0)),
            pl.BlockSpec((tm, LANE), lambda i, j: (i, 0)),
            pl.BlockSpec((1, MLA_RANK), lambda i, j: (0, 0)),
            pl.BlockSpec((1, MLA_RANK), lambda i, j: (0, 0)),
        ],
        out_specs=pl.BlockSpec((tm, EV_BLOCK), lambda i, j: (i, j)),
        out_shape=jax.ShapeDtypeStruct((m, EV_MAIN_COLS), F32),
        scratch_shapes=[pltpu.VMEM((tm, d), BF16)],
        compiler_params=_cp("parallel", "arbitrary"),
        name="even_in_proj",
    )(x, w_main, cr, sr, gq.reshape(1, -1), gkv.reshape(1, -1))


def _kpe_kernel(x_ref, w_ref, t1_ref, t2_ref, o_ref):
    acc = _dot(x_ref[...].astype(BF16), w_ref[...])
    o_ref[...] = acc * t1_ref[...] + _half_swap(acc) * t2_ref[...]


def _kpe_proj(x, w_kpe, t1, t2):
    m, d = x.shape
    tm = _pick(m, ROW_TILES)
    return pl.pallas_call(
        _kpe_kernel,
        grid=(m // tm,),
        in_specs=[
            pl.BlockSpec((tm, d), lambda i: (i, 0)),
            pl.BlockSpec((d, LANE), lambda i: (0, 0)),
            pl.BlockSpec((tm, LANE), lambda i: (i, 0)),
            pl.BlockSpec((tm, LANE), lambda i: (i, 0)),
        ],
        out_specs=pl.BlockSpec((tm, LANE), lambda i: (i, 0)),
        out_shape=jax.ShapeDtypeStruct((m, LANE), F32),
        compiler_params=_cp("parallel"),
        name="kpe_proj",
    )(x, w_kpe, t1, t2)


MLA_QW = 2 * LANE


def _uq_kernel(cq_ref, w_ref, t1_ref, t2_ref, o_ref):
    acc = _dot(cq_ref[...].astype(BF16), w_ref[...])
    t1, t2 = t1_ref[...], t2_ref[...]
    for h in range(MLA_HEADS):
        c0 = h * MLA_QW
        o_ref[:, c0:c0 + LANE] = acc[:, c0:c0 + LANE].astype(BF16)
        x = acc[:, c0 + LANE:c0 + MLA_QW]
        o_ref[:, c0 + LANE:c0 + MLA_QW] = (x * t1 + _half_swap(x) * t2).astype(BF16)


def _uq_proj(proj, w_uq_ext, t1, t2):
    m = proj.shape[0]
    tm = _pick(m, ROW_TILES[:-1])
    n = MLA_HEADS * MLA_QW
    return pl.pallas_call(
        _uq_kernel,
        grid=(m // tm,),
        in_specs=[
            pl.BlockSpec((tm, MLA_RANK), lambda i: (i, EV_CQ_BLOCK)),
            pl.BlockSpec((MLA_RANK, n), lambda i: (0, 0)),
            pl.BlockSpec((tm, LANE), lambda i: (i, 0)),
            pl.BlockSpec((tm, LANE), lambda i: (i, 0)),
        ],
        out_specs=pl.BlockSpec((tm, n), lambda i: (i, 0)),
        out_shape=jax.ShapeDtypeStruct((m, n), BF16),
        compiler_params=_cp("parallel"),
        name="mla_q_up_proj",
    )(proj, w_uq_ext, t1, t2)


def _ukv_kernel(ckv_ref, w_ref, kpe_ref, k_ref, v_ref):
    acc = _dot(ckv_ref[...].astype(BF16), w_ref[...])
    kpe = kpe_ref[...].astype(BF16)
    for h in range(MLA_HEADS):
        c0 = h * MLA_QW
        k_ref[:, c0:c0 + LANE] = acc[:, c0:c0 + LANE].astype(BF16)
        k_ref[:, c0 + LANE:c0 + MLA_QW] = kpe
        v_ref[:, h * MLA_V:(h + 1) * MLA_V] = acc[:, c0 + LANE:c0 + MLA_QW].astype(BF16)


def _ukv_proj(proj, w_ukv, kpe_pad, mp):
    tm = _pick(mp, ROW_TILES[:-1])
    n = MLA_HEADS * MLA_QW
    return pl.pallas_call(
        _ukv_kernel,
        grid=(mp // tm,),
        in_specs=[
            pl.BlockSpec((tm, MLA_RANK), lambda i: (i, EV_CKV_BLOCK)),
            pl.BlockSpec((MLA_RANK, n), lambda i: (0, 0)),
            pl.BlockSpec((tm, LANE), lambda i: (i, 0)),
        ],
        out_specs=[pl.BlockSpec((tm, n), lambda i: (i, 0)),
                   pl.BlockSpec((tm, MLA_HEADS * MLA_V), lambda i: (i, 0))],
        out_shape=[jax.ShapeDtypeStruct((mp, n), BF16),
                   jax.ShapeDtypeStruct((mp, MLA_HEADS * MLA_V), BF16)],
        compiler_params=_cp("parallel"),
        name="mla_kv_up_proj",
    )(proj, w_ukv, kpe_pad)


def _ret_core(q, k, v, g, s0, dec, qd, kd, cd, gn_g, gn_b):
    qb, kb, vb = q.astype(BF16), k.astype(BF16), v.astype(BF16)
    s = _dot_nt(qb, kb) * dec
    o = _dot(s.astype(BF16), vb)
    o = o + _dot(qb, s0.astype(BF16)) * qd
    kk_t = (k * kd).T.astype(BF16)
    s_new = s0 * cd + _dot(kk_t, vb)
    on = _layer_norm(o, gn_g, gn_b)
    return on * (g * jax.nn.sigmoid(g)), s_new


def _ret_prompt_kernel(q_ref, k_ref, v_ref, g_ref, dec_ref, qd_ref, kd_ref, cd_ref, gg_ref, gb_ref,
                       o_ref, s_out_ref, s_ref, *, nc):
    c = pl.program_id(2)

    @pl.when(c == 0)
    def _():
        s_ref[...] = jnp.zeros_like(s_ref)

    for h in range(RET_HEAD_GROUP):
        cols = slice(h * RET_D, (h + 1) * RET_D)
        out, s_new = _ret_core(q_ref[:, cols], k_ref[:, cols], v_ref[:, cols], g_ref[:, cols], s_ref[h],
                               dec_ref[h], qd_ref[h], kd_ref[h], cd_ref[h], gg_ref[h], gb_ref[h])
        o_ref[:, cols] = out
        s_ref[h] = s_new

    @pl.when(c == nc - 1)
    def _():
        s_out_ref[...] = s_ref[...]


RET_HEAD_GROUP = 4


def _ret_prompt(proj, tabs, gn_g, gn_b, batch, seq):
    nc = seq // RET_CHUNK
    hg = RET_HEAD_GROUP
    gw = hg * RET_D
    groups = RET_HEADS // hg

    def tok(section):
        return pl.BlockSpec((RET_CHUNK, gw), lambda b, h, c: (b * nc + c, section * groups + h))

    def per_head(shape):
        return pl.BlockSpec((hg,) + shape, lambda b, h, c: (h, 0, 0))

    dec, qd, kd, cd = tabs
    return pl.pallas_call(
        functools.partial(_ret_prompt_kernel, nc=nc),
        grid=(batch, groups, nc),
        in_specs=[tok(0), tok(1), tok(2), tok(3),
                  per_head((RET_CHUNK, RET_CHUNK)), per_head((RET_CHUNK, LANE)), per_head((RET_CHUNK, LANE)),
                  per_head((1, LANE)), per_head((1, RET_D)), per_head((1, RET_D))],
        out_specs=[pl.BlockSpec((RET_CHUNK, gw), lambda b, h, c: (b * nc + c, h)),
                   pl.BlockSpec((None, hg, RET_D, RET_D), lambda b, h, c: (b, h, 0, 0))],
        out_shape=[jax.ShapeDtypeStruct((batch * seq, RET_HEADS * RET_D), F32),
                   jax.ShapeDtypeStruct((batch, RET_HEADS, RET_D, RET_D), F32)],
        scratch_shapes=[pltpu.VMEM((hg, RET_D, RET_D), F32)],
        compiler_params=_cp("parallel", "parallel", "arbitrary"),
        name="retention_prompt",
    )(proj, proj, proj, proj, dec, qd, kd, cd, gn_g[:, None, :], gn_b[:, None, :])


def _ret_sample_kernel(q_ref, k_ref, v_ref, g_ref, s0_ref, dec_ref, qd_ref, kd_ref, cd_ref, gg_ref, gb_ref,
                       o_ref, s_out_ref, *, ts):
    pad = jnp.zeros((RET_CHUNK - ts, RET_D), F32)

    def padded(ref, h):
        return jnp.concatenate([ref[:, h * RET_D:(h + 1) * RET_D], pad], axis=0)

    for h in range(RET_HEADS):
        out, s_new = _ret_core(padded(q_ref, h), padded(k_ref, h), padded(v_ref, h), padded(g_ref, h),
                               s0_ref[h], dec_ref[h], qd_ref[h], kd_ref[h], cd_ref[h], gg_ref[h], gb_ref[h])
        o_ref[:, h * RET_D:(h + 1) * RET_D] = out[:ts]
        s_out_ref[h] = s_new


def _ret_sample(proj, state, tabs, gn_g, gn_b, mp, bd, ts):
    width = RET_HEADS * RET_D
    row0 = mp // ts

    def tok(j):
        return pl.BlockSpec((ts, width), lambda b: (row0 + b, j))

    def const(a):
        return pl.BlockSpec(a.shape, lambda b: (0,) * a.ndim)

    dec, qd, kd, cd = tabs
    gg, gb = gn_g[:, None, :], gn_b[:, None, :]
    return pl.pallas_call(
        functools.partial(_ret_sample_kernel, ts=ts),
        grid=(bd,),
        in_specs=[tok(0), tok(1), tok(2), tok(3),
                  pl.BlockSpec((None, RET_HEADS, RET_D, RET_D), lambda b: (b, 0, 0, 0)),
                  const(dec), const(qd), const(kd), const(cd), const(gg), const(gb)],
        out_specs=[pl.BlockSpec((ts, width), lambda b: (b, 0)),
                   pl.BlockSpec((None, RET_HEADS, RET_D, RET_D), lambda b: (b, 0, 0, 0))],
        out_shape=[jax.ShapeDtypeStruct((bd * ts, width), F32),
                   jax.ShapeDtypeStruct(state.shape, F32)],
        compiler_params=_cp("parallel"),
        name="retention_sample",
    )(proj, proj, proj, proj, state, dec, qd, kd, cd, gg, gb)


def _ret_tables(chunk_len):
    h = jnp.arange(RET_HEADS, dtype=F32)
    lg = jnp.log(1.0 - jnp.exp2(-5.0 - h))
    idx = jnp.arange(RET_CHUNK, dtype=F32)
    diff = idx[:, None] - idx[None, :]
    dec = jnp.where(diff >= 0, jnp.exp(jnp.maximum(diff, 0.0)[None] * lg[:, None, None]), 0.0)
    qd = jnp.exp((idx + 1.0)[None, :] * lg[:, None])
    kd = jnp.exp((chunk_len - 1.0 - idx)[None, :] * lg[:, None])
    kd = jnp.where(idx[None, :] < chunk_len, kd, 0.0)
    cd = jnp.exp(chunk_len * lg)
    rep = lambda a: jnp.broadcast_to(a[..., None], a.shape + (LANE,))
    return dec, rep(qd), rep(kd), rep(cd[:, None])


def _mla_prompt_kernel(q_ref, k_ref, v_ref, o_ref, *, tq):
    i = pl.program_id(2)
    q = q_ref[...]

    def step(j, carry, masked):
        m, l, acc = carry
        rows = pl.ds(pl.multiple_of(j * tq, tq), tq)
        s = _dot_nt(q, k_ref[rows, :]) * MLA_SCALE
        if masked:
            r = lax.broadcasted_iota(jnp.int32, (tq, tq), 0)
            c = lax.broadcasted_iota(jnp.int32, (tq, tq), 1)
            s = jnp.where(c <= r, s, NEG)
        m_new = jnp.maximum(m, jnp.max(s, axis=1, keepdims=True))
        a = jnp.exp(m - m_new)
        p = jnp.exp(s - m_new)
        l = a * l + jnp.sum(p, axis=1, keepdims=True)
        acc = a * acc + _dot(p.astype(BF16), v_ref[rows, :])
        return m_new, l, acc

    init = (jnp.full((tq, 1), NEG, F32), jnp.zeros((tq, 1), F32), jnp.zeros((tq, MLA_V), F32))
    carry = lax.fori_loop(0, i, lambda j, c: step(j, c, False), init)
    m, l, acc = step(i, carry, True)
    o_ref[...] = acc / l


def _mla_prompt(q_cat, k_cat, v, batch, seq):
    tq = _pick(seq, (512, 256, 128))
    nq = seq // tq
    return pl.pallas_call(
        functools.partial(_mla_prompt_kernel, tq=tq),
        grid=(batch, MLA_HEADS, nq),
        in_specs=[pl.BlockSpec((tq, MLA_QW), lambda b, h, i: (b * nq + i, h)),
                  pl.BlockSpec((seq, MLA_QW), lambda b, h, i: (b, h)),
                  pl.BlockSpec((seq, MLA_V), lambda b, h, i: (b, h))],
        out_specs=pl.BlockSpec((tq, MLA_V), lambda b, h, i: (b * nq + i, h)),
        out_shape=jax.ShapeDtypeStruct((batch * seq, MLA_HEADS * MLA_V), F32),
        compiler_params=_cp("parallel", "parallel", "arbitrary"),
        name="mla_prompt_attention",
    )(q_cat, k_cat, v)


def _qlat_kernel(q_ref, w_ref, o_ref):
    o_ref[...] = _dot(q_ref[...], w_ref[...])


def _q_latent(q_cat, w_uk_t, mp, ms):
    tm = _pick(ms, ROW_TILES[:-1])
    row0 = mp // tm
    return pl.pallas_call(
        _qlat_kernel,
        grid=(MLA_HEADS, ms // tm),
        in_specs=[pl.BlockSpec((tm, MLA_NOPE), lambda h, i: (row0 + i, 2 * h)),
                  pl.BlockSpec((None, MLA_NOPE, MLA_RANK), lambda h, i: (h, 0, 0))],
        out_specs=pl.BlockSpec((None, tm, MLA_RANK), lambda h, i: (h, i, 0)),
        out_shape=jax.ShapeDtypeStruct((MLA_HEADS, ms, MLA_RANK), F32),
        compiler_params=_cp("parallel", "parallel"),
        name="mla_q_latent",
    )(q_cat, w_uk_t)


def _mla_sample_kernel(pt_ref, ql_ref, qp_ref, ckvn_ref, kpen_ref, ckv_hbm, kpe_hbm, o_ref,
                       ckv_buf, kpe_buf, kv_ref, kp_ref, sem, *, layer, npg, nchunk, ts):
    b = pl.program_id(0)
    nb = pl.num_programs(0)
    n_pages = npg * nchunk
    rows = MLA_HEADS * ts

    def slot_of(c):
        return c % 2 if nchunk % 2 == 0 else lax.rem(b * nchunk + c, 2)

    def chunk_copies(seq, c, slot):
        cps = []
        for j in range(npg):
            page = pt_ref[seq * n_pages + c * npg + j]
            cps.append(pltpu.make_async_copy(ckv_hbm.at[layer, page], ckv_buf.at[slot, pl.ds(j * PAGE, PAGE)],
                                             sem.at[0, slot]))
            cps.append(pltpu.make_async_copy(kpe_hbm.at[layer, page], kpe_buf.at[slot, j], sem.at[1, slot]))
        return cps

    @pl.when(b == 0)
    def _():
        for cp in chunk_copies(b, 0, slot_of(0)):
            cp.start()

    ql = ql_ref[...].reshape(rows, MLA_RANK).astype(BF16)
    qp = qp_ref[...].astype(BF16)

    def update(state, s, v):
        m, l, acc = state
        m_new = jnp.maximum(m, jnp.max(s, axis=1, keepdims=True))
        a = jnp.exp(m - m_new)
        p = jnp.exp(s - m_new)
        return m_new, a * l + jnp.sum(p, axis=1, keepdims=True), a * acc + _dot(p.astype(BF16), v)

    state = (jnp.full((rows, 1), NEG, F32), jnp.zeros((rows, 1), F32), jnp.zeros((rows, MLA_RANK), F32))
    nxt_seq = jnp.minimum(b + 1, nb - 1)
    cur = chunk_copies(b, 0, slot_of(0))
    for c in range(nchunk):
        slot = slot_of(c)
        nxt = chunk_copies(b, c + 1, 1 - slot) if c + 1 < nchunk else chunk_copies(nxt_seq, 0, 1 - slot)
        for cp in nxt:
            cp.start()
        for cp in cur:
            cp.wait()
        kvb, kpb = kv_ref.at[c % 2], kp_ref.at[c % 2]
        kvb[...] = ckv_buf[slot].astype(BF16)
        for j in range(npg):
            kpb[:, j * PAGE:(j + 1) * PAGE] = kpe_buf[slot, j].astype(BF16)
        kv = kvb[...]
        state = update(state, (_dot_nt(ql, kv) + _dot(qp, kpb[...])) * MLA_SCALE, kv)
        cur = nxt

    ckv = jnp.concatenate([ckvn_ref[...], jnp.zeros((PAGE - ts, MLA_RANK), F32)], axis=0).astype(BF16)
    kpe = jnp.concatenate([kpen_ref[...], jnp.zeros((PAGE - ts, MLA_ROPE), F32)], axis=0).astype(BF16)
    s = (_dot_nt(ql, ckv) + _dot_nt(qp, kpe)) * MLA_SCALE
    t = lax.broadcasted_iota(jnp.int32, (rows, PAGE), 0) % ts
    key = lax.broadcasted_iota(jnp.int32, (rows, PAGE), 1)
    _, l, acc = update(state, jnp.where(key <= t, s, NEG), ckv)
    o_ref[...] = acc / l

    @pl.when(b == nb - 1)
    def _():
        for cp in cur:
            cp.wait()


MLA_CHUNK_PAGES = (16, 8, 4, 2, 1)


def _mla_sample(page_table_flat, q_lat, q_pe, ckv_new, kpe_new, cache_ckv, cache_kpe_t, layer, bd, ts, n_pages):
    npg = _pick(n_pages, MLA_CHUNK_PAGES)
    nchunk = n_pages // npg
    rows = MLA_HEADS * ts
    in_specs = [pl.BlockSpec((MLA_HEADS, ts, MLA_RANK), lambda b, pt: (0, b, 0)),
                pl.BlockSpec((None, rows, MLA_ROPE), lambda b, pt: (b, 0, 0)),
                pl.BlockSpec((None, ts, MLA_RANK), lambda b, pt: (b, 0, 0)),
                pl.BlockSpec((None, ts, MLA_ROPE), lambda b, pt: (b, 0, 0)),
                pl.BlockSpec(memory_space=pl.ANY),
                pl.BlockSpec(memory_space=pl.ANY)]
    return pl.pallas_call(
        functools.partial(_mla_sample_kernel, layer=layer, npg=npg, nchunk=nchunk, ts=ts),
        grid_spec=pltpu.PrefetchScalarGridSpec(
            num_scalar_prefetch=1,
            grid=(bd,),
            in_specs=in_specs,
            out_specs=pl.BlockSpec((None, rows, MLA_RANK), lambda b, pt: (b, 0, 0)),
            scratch_shapes=[pltpu.VMEM((2, npg * PAGE, MLA_RANK), F32), pltpu.VMEM((2, npg, MLA_ROPE, PAGE), F32),
                            pltpu.VMEM((2, npg * PAGE, MLA_RANK), BF16), pltpu.VMEM((2, MLA_ROPE, npg * PAGE), BF16),
                            pltpu.SemaphoreType.DMA((2, 2))],
        ),
        out_shape=jax.ShapeDtypeStruct((bd, rows, MLA_RANK), F32),
        compiler_params=_cp("arbitrary"),
        name="mla_sample_attention",
    )(page_table_flat, q_lat, q_pe, ckv_new, kpe_new, cache_ckv, cache_kpe_t)


def _uv_kernel(o_ref, w_ref, out_ref, *, ts):
    tb = o_ref.shape[0]
    o = o_ref[...].reshape(tb * ts, MLA_RANK).astype(BF16)
    out_ref[...] = _dot(o, w_ref[...])


def _mla_value_up(o_lat, w_uv, bd, ts):
    tb = _pick(bd, (64, 32, 16, 8, 4, 2, 1))
    return pl.pallas_call(
        functools.partial(_uv_kernel, ts=ts),
        grid=(MLA_HEADS, bd // tb),
        in_specs=[pl.BlockSpec((tb, None, ts, MLA_RANK), lambda h, i: (i, h, 0, 0)),
                  pl.BlockSpec((None, MLA_RANK, MLA_V), lambda h, i: (h, 0, 0))],
        out_specs=pl.BlockSpec((tb * ts, MLA_V), lambda h, i: (i, h)),
        out_shape=jax.ShapeDtypeStruct((bd * ts, MLA_HEADS * MLA_V), F32),
        compiler_params=_cp("parallel", "parallel"),
        name="mla_value_up_proj",
    )(o_lat, w_uv)


def _mm_kernel(x_ref, w_ref, o_ref, xb_ref):
    @pl.when(pl.program_id(1) == 0)
    def _():
        xb_ref[...] = x_ref[...].astype(BF16)

    o_ref[...] = _dot(xb_ref[...], w_ref[...])


def _matmul(x, w):
    m, d = x.shape
    n = w.shape[1]
    tm = _pick(m, PROJ_ROW_TILES)
    tn = _pick(n, (512, 256, 128))
    return pl.pallas_call(
        _mm_kernel,
        grid=(m // tm, n // tn),
        in_specs=[pl.BlockSpec((tm, d), lambda i, j: (i, 0)),
                  pl.BlockSpec((d, tn), lambda i, j: (0, j))],
        out_specs=pl.BlockSpec((tm, tn), lambda i, j: (i, j)),
        out_shape=jax.ShapeDtypeStruct((m, n), F32),
        scratch_shapes=[pltpu.VMEM((tm, d), BF16)],
        compiler_params=_cp("parallel", "arbitrary"),
        name="sb_in_proj",
    )(x, w)


SB_Q_COLS = SB_HEADS * SB_DIM
SB_KV_COLS = SB_KV_HEADS * SB_DIM
SB_TK = 128
SB_EXIT = -104.0


def _sb_block(z, mask, carry, tri_ones):
    r = z.shape[0]
    sp = jnp.maximum(z, 0.0) + jnp.log1p(jnp.exp(-jnp.abs(z)))
    if mask is not None:
        sp_m = jnp.where(mask, sp, 0.0)
    else:
        sp_m = sp
    hi = sp_m.astype(BF16)
    lo = (sp_m - hi.astype(F32)).astype(BF16)
    cs = _dot(jnp.concatenate([hi, lo], axis=0), tri_ones)
    cs = cs[:r] + cs[r:]
    a = jnp.exp(z - sp + carry - cs[:, :SB_TK])
    if mask is not None:
        a = jnp.where(mask, a, 0.0)
    return a, carry - cs[:, SB_TK:]


def _sb_live(carry):
    return (jnp.max(carry) > SB_EXIT).astype(jnp.int32)


def _sb_prompt_kernel(q_ref, k_ref, v_ref, to_ref, o_ref, *, tq):
    i = pl.program_id(2)
    rows = SB_GROUP * tq
    q = jnp.concatenate([q_ref[:, g * SB_DIM:(g + 1) * SB_DIM] for g in range(SB_GROUP)], axis=0).astype(BF16)
    tri_ones = to_ref[...]

    def step(j, carry, acc, mask):
        ks = pl.ds(pl.multiple_of(j * SB_TK, SB_TK), SB_TK)
        z = _dot_nt(q, k_ref[ks, :].astype(BF16)) * SB_SCALE
        a, carry = _sb_block(z, mask, carry, tri_ones)
        return carry, acc + _dot(a.astype(BF16), v_ref[ks, :].astype(BF16))

    t = lax.broadcasted_iota(jnp.int32, (rows, SB_TK), 0) % tq
    key = lax.broadcasted_iota(jnp.int32, (rows, SB_TK), 1)
    carry, acc = step(i, jnp.zeros((rows, SB_TK), F32), jnp.zeros((rows, SB_DIM), F32), key < t)

    def body(state):
        j, _, carry, acc = state
        carry, acc = step(j, carry, acc, None)
        return j - 1, _sb_live(carry), carry, acc

    _, _, carry, acc = lax.while_loop(lambda s: (s[0] >= 0) & (s[1] > 0), body,
                                      (i - 1, _sb_live(carry), carry, acc))
    for g in range(SB_GROUP):
        o_ref[:, g * SB_DIM:(g + 1) * SB_DIM] = acc[g * tq:(g + 1) * tq]


def _sb_prompt(proj, tri_ones, batch, seq):
    tq = SB_TK
    nq = seq // tq
    gw = SB_GROUP * SB_DIM
    kb0 = SB_Q_COLS // SB_DIM
    vb0 = kb0 + SB_KV_HEADS
    return pl.pallas_call(
        functools.partial(_sb_prompt_kernel, tq=tq),
        grid=(batch, SB_KV_HEADS, nq),
        in_specs=[pl.BlockSpec((tq, gw), lambda b, h, i: (b * nq + i, h)),
                  pl.BlockSpec((seq, SB_DIM), lambda b, h, i: (b, kb0 + h)),
                  pl.BlockSpec((seq, SB_DIM), lambda b, h, i: (b, vb0 + h)),
                  pl.BlockSpec((SB_TK, 2 * SB_TK), lambda b, h, i: (0, 0))],
        out_specs=pl.BlockSpec((tq, gw), lambda b, h, i: (b * nq + i, h)),
        out_shape=jax.ShapeDtypeStruct((batch * seq, SB_Q_COLS), F32),
        compiler_params=_cp("parallel", "parallel", "arbitrary"),
        name="sb_prompt_attention",
    )(proj, proj, proj, tri_ones)


def _sb_sample_kernel(pt_ref, q_ref, kn_ref, vn_ref, to_ref, kc_ref, vc_ref, o_ref, kbuf, vbuf, sem,
                      *, ts, n_pages):
    b = pl.program_id(0)
    hrows = SB_GROUP * ts
    rows = SB_KV_HEADS * hrows
    q = jnp.concatenate([q_ref[:, h * SB_DIM:(h + 1) * SB_DIM] for h in range(SB_HEADS)], axis=0).astype(BF16)
    tri_ones = to_ref[...]

    def page_copies(p, slot):
        page = pt_ref[b * n_pages + p]
        return (pltpu.make_async_copy(kc_ref.at[page], kbuf.at[slot], sem.at[0, slot]),
                pltpu.make_async_copy(vc_ref.at[page], vbuf.at[slot], sem.at[1, slot]))

    def slot_of(p):
        return lax.rem(n_pages - 1 - p, 2)

    def block(k_of, v_of, mask, carry, acc):
        z = jnp.concatenate(
            [_dot_nt(q[h * hrows:(h + 1) * hrows], k_of(h).astype(BF16)) for h in range(SB_KV_HEADS)],
            axis=0) * SB_SCALE
        a, carry = _sb_block(z, mask, carry, tri_ones)
        ab = a.astype(BF16)
        upd = jnp.concatenate(
            [_dot(ab[h * hrows:(h + 1) * hrows], v_of(h).astype(BF16)) for h in range(SB_KV_HEADS)],
            axis=0)
        return carry, acc + upd

    for cp in page_copies(n_pages - 1, 0):
        cp.start()

    zpad = jnp.zeros((PAGE - ts, SB_KV_COLS), F32)
    kn = jnp.concatenate([kn_ref[...], zpad], axis=0)
    vn = jnp.concatenate([vn_ref[...], zpad], axis=0)
    t = lax.broadcasted_iota(jnp.int32, (rows, SB_TK), 0) % ts
    key = lax.broadcasted_iota(jnp.int32, (rows, SB_TK), 1)
    carry, acc = block(lambda h: kn[:, h * SB_DIM:(h + 1) * SB_DIM], lambda h: vn[:, h * SB_DIM:(h + 1) * SB_DIM],
                       key < t, jnp.zeros((rows, SB_TK), F32), jnp.zeros((rows, SB_DIM), F32))

    def body(state):
        p, _, carry, acc = state
        slot = slot_of(p)
        for cp in page_copies(p, slot):
            cp.wait()

        @pl.when(p > 0)
        def _():
            for cp in page_copies(p - 1, 1 - slot):
                cp.start()

        def head_rows(buf, h):
            return buf[slot, pl.ds(h, PAGE, stride=SB_KV_HEADS), :]

        carry, acc = block(lambda h: head_rows(kbuf, h), lambda h: head_rows(vbuf, h), None, carry, acc)
        return p - 1, _sb_live(carry), carry, acc

    p, _, carry, acc = lax.while_loop(lambda s: (s[0] >= 0) & (s[1] > 0), body,
                                      (jnp.int32(n_pages - 1), _sb_live(carry), carry, acc))

    @pl.when(p >= 0)
    def _():
        for cp in page_copies(p, slot_of(p)):
            cp.wait()

    for h in range(SB_HEADS):
        o_ref[:, h * SB_DIM:(h + 1) * SB_DIM] = acc[h * ts:(h + 1) * ts]


def _sb_sample(page_table_flat, proj, tri_ones, cache_k, cache_v, mp, bd, ts, n_pages):
    row0 = mp // ts
    kb0 = SB_Q_COLS // SB_KV_COLS
    page_rows = PAGE * SB_KV_HEADS
    in_specs = [pl.BlockSpec((ts, SB_Q_COLS), lambda b, pt: (row0 + b, 0)),
                pl.BlockSpec((ts, SB_KV_COLS), lambda b, pt: (row0 + b, kb0)),
                pl.BlockSpec((ts, SB_KV_COLS), lambda b, pt: (row0 + b, kb0 + 1)),
                pl.BlockSpec((SB_TK, 2 * SB_TK), lambda b, pt: (0, 0)),
                pl.BlockSpec(memory_space=pl.ANY),
                pl.BlockSpec(memory_space=pl.ANY)]
    return pl.pallas_call(
        functools.partial(_sb_sample_kernel, ts=ts, n_pages=n_pages),
        grid_spec=pltpu.PrefetchScalarGridSpec(
            num_scalar_prefetch=1,
            grid=(bd,),
            in_specs=in_specs,
            out_specs=pl.BlockSpec((ts, SB_Q_COLS), lambda b, pt: (b, 0)),
            scratch_shapes=[pltpu.VMEM((2, page_rows, SB_DIM), F32), pltpu.VMEM((2, page_rows, SB_DIM), F32),
                            pltpu.SemaphoreType.DMA((2, 2))],
        ),
        out_shape=jax.ShapeDtypeStruct((bd * ts, SB_Q_COLS), F32),
        compiler_params=_cp("arbitrary"),
        name="sb_sample_attention",
    )(page_table_flat, proj, proj, proj, tri_ones, cache_k, cache_v)


def _rope_tables(pos):
    p = pos.astype(F32)[:, None]
    inv_r = 1.0 / (ROPE_BASE ** jnp.linspace(0.0, 1.0, RET_D // 2, dtype=F32))
    ang = p * inv_r[None, :]
    c, s = jnp.cos(ang), jnp.sin(ang)
    cr = jnp.concatenate([c, c], axis=1)
    sr = jnp.concatenate([-s, s], axis=1)
    inv_m = 1.0 / (ROPE_BASE ** (jnp.arange(0, MLA_ROPE, 2, dtype=F32) / MLA_ROPE))
    ang = p * inv_m[None, :]
    c, s = jnp.cos(ang), jnp.sin(ang)
    z = jnp.zeros((pos.shape[0], LANE - MLA_ROPE), F32)
    t1 = jnp.concatenate([c, c, z], axis=1)
    t2 = jnp.concatenate([-s, s, z], axis=1)
    return cr, sr, t1, t2


def _swap_halves(w):
    half = w.shape[-1] // 2
    return jnp.concatenate([w[..., half:], w[..., :half]], axis=-1)


def kernel(x_prompt, x_sample, state_ret, cache_mla_ckv, cache_mla_kpe, cache_sb_k, cache_sb_v, page_table,
           ln_g, ln_b, ffn_w_gate, ffn_w_up, ffn_w_down,
           ev_w_in, ev_q_norm, ev_w_uq, ev_kv_norm, ev_w_ukv, ev_gn_g, ev_gn_b, ev_w_o,
           od_w_in, od_w_o):
    batch, seq, d = x_prompt.shape
    bd, ts, _ = x_sample.shape
    n_pages = page_table.shape[1]
    past = n_pages * PAGE
    mp, ms = batch * seq, bd * ts
    assert seq % RET_CHUNK == 0 and ts <= 8 and mp % ts == 0

    x = jnp.concatenate([x_prompt.reshape(mp, d), x_sample.reshape(ms, d)], axis=0)
    pos = jnp.concatenate([jnp.tile(jnp.arange(seq), batch), jnp.tile(past + jnp.arange(ts), bd)])
    cr, sr, t1, t2 = _rope_tables(pos)
    pt_flat = page_table.reshape(-1)
    tri = (jnp.arange(SB_TK)[:, None] > jnp.arange(SB_TK)[None, :]).astype(BF16)
    tri_ones = jnp.concatenate([tri, jnp.ones((SB_TK, SB_TK), BF16)], axis=1)
    tabs_p = _ret_tables(float(RET_CHUNK))
    tabs_s = _ret_tables(float(ts))

    wg, wu, wd = ffn_w_gate.astype(BF16), ffn_w_up.astype(BF16), ffn_w_down.astype(BF16)

    outs = {k: [] for k in ("ret_p", "ret_s", "ckv_p", "ckv_s", "kpe_p", "kpe_s", "sbk_p", "sbk_s", "sbv_p", "sbv_s")}
    for l in range(DEPTH):
        x = _ffn_half_step(x, wg, wu, wd, l, 0, ln_g[l, 0], ln_b[l, 0])
        if l % 2 == 0:
            e = l // 2
            w_in = ev_w_in[e]
            w_main = w_in[:, :EV_MAIN_COLS].astype(BF16)
            w_kpe = w_in[:, EV_MAIN_COLS:]
            w_kpe = jnp.concatenate([w_kpe, _swap_halves(w_kpe)], axis=1).astype(BF16)
            proj = _ev_proj(x, w_main, cr, sr, ev_q_norm[e], ev_kv_norm[e])
            kpe_pad = _kpe_proj(x, w_kpe, t1, t2)

            w_uq = ev_w_uq[e].reshape(MLA_RANK, MLA_HEADS, MLA_NOPE + MLA_ROPE)
            w_uq_ext = jnp.concatenate([w_uq, _swap_halves(w_uq[..., MLA_NOPE:])], axis=-1)
            q_cat = _uq_proj(proj, w_uq_ext.reshape(MLA_RANK, MLA_HEADS * MLA_QW).astype(BF16), t1, t2)

            w_ukv = ev_w_ukv[e]
            k_cat, v_p = _ukv_proj(proj, w_ukv.astype(BF16), kpe_pad, mp)

            ret_out_p, s_p = _ret_prompt(proj, tabs_p, ev_gn_g[e], ev_gn_b[e], batch, seq)
            ret_out_s, s_s = _ret_sample(proj, state_ret[e], tabs_s, ev_gn_g[e], ev_gn_b[e], mp, bd, ts)
            mla_out_p = _mla_prompt(q_cat, k_cat, v_p, batch, seq)

            w3 = w_ukv.reshape(MLA_RANK, MLA_HEADS, MLA_NOPE + MLA_V)
            w_uk_t = w3[..., :MLA_NOPE].transpose(1, 2, 0).astype(BF16)
            w_uv = w3[..., MLA_NOPE:].transpose(1, 0, 2).astype(BF16)
            q_lat = _q_latent(q_cat, w_uk_t, mp, ms)
            q_pe_s = q_cat[mp:].reshape(bd, ts, MLA_HEADS, MLA_QW)[..., MLA_NOPE:MLA_NOPE + MLA_ROPE]
            q_pe_s = q_pe_s.transpose(0, 2, 1, 3).reshape(bd, MLA_HEADS * ts, MLA_ROPE)
            ckv = proj[:, EV_CKV_BLOCK * EV_BLOCK:(EV_CKV_BLOCK + 1) * EV_BLOCK]
            kpe = kpe_pad[:, :MLA_ROPE]
            ckv_s = ckv[mp:].reshape(bd, ts, MLA_RANK)
            kpe_s = kpe[mp:].reshape(bd, ts, MLA_ROPE)
            o_lat = _mla_sample(pt_flat, q_lat, q_pe_s, ckv_s, kpe_s, cache_mla_ckv,
                                jnp.swapaxes(cache_mla_kpe, 2, 3), e, bd, ts, n_pages)
            mla_out_s = _mla_value_up(o_lat.reshape(bd, MLA_HEADS, ts, MLA_RANK), w_uv, bd, ts)

            w_o = ev_w_o[e].astype(BF16)
            n_ret = RET_HEADS * RET_D
            x = _out_proj_ln([jnp.concatenate([ret_out_p, ret_out_s], axis=0),
                              jnp.concatenate([mla_out_p, mla_out_s], axis=0)],
                             [w_o[:n_ret], w_o[n_ret:]], x, ln_g[l, 1], ln_b[l, 1])
            outs["ret_p"].append(s_p)
            outs["ret_s"].append(s_s)
            outs["ckv_p"].append(ckv[:mp].reshape(batch, seq, MLA_RANK))
            outs["ckv_s"].append(ckv_s)
            outs["kpe_p"].append(kpe[:mp].reshape(batch, seq, MLA_ROPE))
            outs["kpe_s"].append(kpe_s)
        else:
            o = l // 2
            proj = _matmul(x, od_w_in[o].astype(BF16))
            n_kv = cache_sb_k.shape[1]
            cache_k = cache_sb_k[o].reshape(n_kv, PAGE * SB_KV_HEADS, SB_DIM)
            cache_v = cache_sb_v[o].reshape(n_kv, PAGE * SB_KV_HEADS, SB_DIM)
            att_p = _sb_prompt(proj, tri_ones, batch, seq)
            att_s = _sb_sample(pt_flat, proj, tri_ones, cache_k, cache_v, mp, bd, ts, n_pages)
            x = _out_proj_ln([jnp.concatenate([att_p, att_s], axis=0)], [od_w_o[o].astype(BF16)],
                             x, ln_g[l, 1], ln_b[l, 1])
            k = proj[:, SB_Q_COLS:SB_Q_COLS + SB_KV_COLS]
            v = proj[:, SB_Q_COLS + SB_KV_COLS:]
            outs["sbk_p"].append(k[:mp].reshape(batch, seq, SB_KV_HEADS, SB_DIM))
            outs["sbk_s"].append(k[mp:].reshape(bd, ts, SB_KV_HEADS, SB_DIM))
            outs["sbv_p"].append(v[:mp].reshape(batch, seq, SB_KV_HEADS, SB_DIM))
            outs["sbv_s"].append(v[mp:].reshape(bd, ts, SB_KV_HEADS, SB_DIM))
        x = _ffn_half_step(x, wg, wu, wd, l, 1, ln_g[l, 2], ln_b[l, 2])

    y_p = x[:mp].reshape(batch, seq, d)
    y_s = x[mp:].reshape(bd, ts, d)
    return (y_p, y_s, jnp.stack(outs["ret_p"]), jnp.stack(outs["ret_s"]),
            jnp.stack(outs["ckv_p"]), jnp.stack(outs["ckv_s"]), jnp.stack(outs["kpe_p"]), jnp.stack(outs["kpe_s"]),
            jnp.stack(outs["sbk_p"]), jnp.stack(outs["sbk_s"]), jnp.stack(outs["sbv_p"]), jnp.stack(outs["sbv_s"]))
```

```python
import functools

import jax
import jax.numpy as jnp
import numpy as np
from jax import lax
from jax.experimental import pallas as pl
from jax.experimental.pallas import tpu as pltpu

F32 = jnp.float32
BF16 = jnp.bfloat16

DEPTH = 2
PAGE = 128
RET_HEADS = 8
RET_D = 128
MLA_HEADS = 8
MLA_NOPE = 128
MLA_ROPE = 64
MLA_V = 128
MLA_RANK = 512
MLA_SCALE = (MLA_NOPE + MLA_ROPE) ** -0.5
SB_HEADS = 16
SB_KV_HEADS = 4
SB_GROUP = SB_HEADS // SB_KV_HEADS
SB_DIM = 128
SB_SCALE = SB_DIM ** -0.5
ROPE_BASE = 10000.0
EPS = 1e-5
ALPHA = (2 * DEPTH) ** 0.25
RET_CHUNK = 128
NEG = -1e30

LANE = 128
VMEM_LIMIT = 56 * 1024 * 1024
ROW_TILES = (512, 256, 128, 64, 32, 16, 8)
FFN_ROW_TILES = (768,) + ROW_TILES
PROJ_ROW_TILES = (1024,) + ROW_TILES


def _pick(n, cands):
    for c in cands:
        if n % c == 0:
            return c
    raise ValueError(f"no tile for {n}")


def _cp(*sem):
    return pltpu.CompilerParams(dimension_semantics=sem, vmem_limit_bytes=VMEM_LIMIT)


def _dot(a, b):
    return jnp.dot(a, b, preferred_element_type=F32)


def _dot_nt(a, b):
    return lax.dot_general(a, b, (((1,), (1,)), ((), ())), preferred_element_type=F32)


def _layer_norm(y, g, b):
    mu = jnp.mean(y, axis=-1, keepdims=True)
    d = y - mu
    var = jnp.mean(d * d, axis=-1, keepdims=True)
    return d * lax.rsqrt(var + EPS) * g + b


def _rms_norm(y, g):
    return y * lax.rsqrt(jnp.mean(y * y, axis=-1, keepdims=True) + EPS) * g


def _half_swap(x):
    return pltpu.roll(x, LANE // 2, 1)


def _ffn_kernel(x_ref, wg_ref, wu_ref, wd_ref, g_ref, b_ref, o_ref, xb_ref, acc_ref, *, nf):
    f = pl.program_id(1)

    @pl.when(f == 0)
    def _():
        xb_ref[...] = x_ref[...].astype(BF16)
        acc_ref[...] = jnp.zeros_like(acc_ref)

    xb = xb_ref[...]
    g = _dot(xb, wg_ref[...])
    u = _dot(xb, wu_ref[...])
    a = (g * jax.nn.sigmoid(g) * u).astype(BF16)
    acc_ref[...] += _dot(a, wd_ref[...])

    @pl.when(f == nf - 1)
    def _():
        y = ALPHA * x_ref[...] + 0.5 * acc_ref[...]
        o_ref[...] = _layer_norm(y, g_ref[...], b_ref[...])


def _ffn_half_step(x, wg, wu, wd, layer, half, g, b):
    m, d = x.shape
    ff = wg.shape[-1]
    tm = _pick(m, FFN_ROW_TILES)
    tf = _pick(ff, (512, 256, 128))
    nf = ff // tf
    return pl.pallas_call(
        functools.partial(_ffn_kernel, nf=nf),
        grid=(m // tm, nf),
        in_specs=[
            pl.BlockSpec((tm, d), lambda i, f: (i, 0)),
            pl.BlockSpec((None, None, d, tf), lambda i, f: (layer, half, 0, f)),
            pl.BlockSpec((None, None, d, tf), lambda i, f: (layer, half, 0, f)),
            pl.BlockSpec((None, None, tf, d), lambda i, f: (layer, half, f, 0)),
            pl.BlockSpec((1, d), lambda i, f: (0, 0)),
            pl.BlockSpec((1, d), lambda i, f: (0, 0)),
        ],
        out_specs=pl.BlockSpec((tm, d), lambda i, f: (i, 0)),
        out_shape=jax.ShapeDtypeStruct((m, d), F32),
        scratch_shapes=[pltpu.VMEM((tm, d), BF16), pltpu.VMEM((tm, d), F32)],
        compiler_params=_cp("parallel", "arbitrary"),
        name="ffn_half_step",
    )(x, wg, wu, wd, g.reshape(1, d), b.reshape(1, d))


def _out_ln_kernel(*refs, n_parts):
    a_refs = refs[:n_parts]
    w_refs = refs[n_parts:2 * n_parts]
    x_ref, g_ref, b_ref, o_ref = refs[2 * n_parts:]
    acc = _dot(a_refs[0][...].astype(BF16), w_refs[0][...])
    for a_ref, w_ref in zip(a_refs[1:], w_refs[1:]):
        acc += _dot(a_ref[...].astype(BF16), w_ref[...])
    y = ALPHA * x_ref[...] + acc
    o_ref[...] = _layer_norm(y, g_ref[...], b_ref[...])


def _out_proj_ln(parts, weights, x, g, b):
    m, d = x.shape
    tm = _pick(m, ROW_TILES)
    n = len(parts)
    in_specs = [pl.BlockSpec((tm, p.shape[1]), lambda i: (i, 0)) for p in parts]
    in_specs += [pl.BlockSpec(w.shape, lambda i: (0, 0)) for w in weights]
    in_specs += [pl.BlockSpec((tm, d), lambda i: (i, 0)),
                 pl.BlockSpec((1, d), lambda i: (0, 0)),
                 pl.BlockSpec((1, d), lambda i: (0, 0))]
    return pl.pallas_call(
        functools.partial(_out_ln_kernel, n_parts=n),
        grid=(m // tm,),
        in_specs=in_specs,
        out_specs=pl.BlockSpec((tm, d), lambda i: (i, 0)),
        out_shape=jax.ShapeDtypeStruct((m, d), F32),
        compiler_params=_cp("parallel"),
        name="out_proj_ln",
    )(*parts, *weights, x, g.reshape(1, d), b.reshape(1, d))


EV_BLOCK = 512
EV_Q_BLOCKS = (0, 2)
EV_K_BLOCKS = (2, 4)
EV_CQ_BLOCK = 8
EV_CKV_BLOCK = 9
EV_MAIN_COLS = 10 * EV_BLOCK


def _ev_proj_kernel(x_ref, w_ref, cr_ref, sr_ref, gq_ref, gkv_ref, o_ref, xb_ref):
    j = pl.program_id(1)

    @pl.when(j == 0)
    def _():
        xb_ref[...] = x_ref[...].astype(BF16)

    acc = _dot(xb_ref[...], w_ref[...])

    def rope(scale):
        c, s = cr_ref[...], sr_ref[...]
        for h in range(EV_BLOCK // LANE):
            blk = acc[:, h * LANE:(h + 1) * LANE]
            r = blk * c + _half_swap(blk) * s
            o_ref[:, h * LANE:(h + 1) * LANE] = r if scale is None else r * scale

    @pl.when(j < EV_Q_BLOCKS[1])
    def _():
        rope(None)

    @pl.when((j >= EV_K_BLOCKS[0]) & (j < EV_K_BLOCKS[1]))
    def _():
        rope(RET_D ** -0.5)

    @pl.when((j >= EV_K_BLOCKS[1]) & (j < EV_CQ_BLOCK))
    def _():
        o_ref[...] = acc

    @pl.when(j == EV_CQ_BLOCK)
    def _():
        o_ref[...] = _rms_norm(acc, gq_ref[...])

    @pl.when(j == EV_CKV_BLOCK)
    def _():
        o_ref[...] = _rms_norm(acc, gkv_ref[...])


def _ev_proj(x, w_main, cr, sr, gq, gkv):
    m, d = x.shape
    tm = _pick(m, PROJ_ROW_TILES)
    nb = EV_MAIN_COLS // EV_BLOCK
    return pl.pallas_call(
        _ev_proj_kernel,
        grid=(m // tm, nb),
        in_specs=[
            pl.BlockSpec((tm, d), lambda i, j: (i, 0)),
            pl.BlockSpec((d, EV_BLOCK), lambda i, j: (0, j)),
            pl.BlockSpec((tm, LANE), lambda i, j: (i, 0)),
            pl.BlockSpec((tm, LANE), lambda i, j: (i, 0)),
            pl.BlockSpec((1, MLA_RANK), lambda i, j: (0, 0)),
            pl.BlockSpec((1, MLA_RANK), lambda i, j: (0, 0)),
        ],
        out_specs=pl.BlockSpec((tm, EV_BLOCK), lambda i, j: (i, j)),
        out_shape=jax.ShapeDtypeStruct((m, EV_MAIN_COLS), F32),
        scratch_shapes=[pltpu.VMEM((tm, d), BF16)],
        compiler_params=_cp("parallel", "arbitrary"),
        name="even_in_proj",
    )(x, w_main, cr, sr, gq.reshape(1, -1), gkv.reshape(1, -1))


def _kpe_kernel(x_ref, w_ref, t1_ref, t2_ref, o_ref):
    acc = _dot(x_ref[...].astype(BF16), w_ref[...])
    o_ref[...] = acc * t1_ref[...] + _half_swap(acc) * t2_ref[...]


def _kpe_proj(x, w_kpe, t1, t2):
    m, d = x.shape
    tm = _pick(m, ROW_TILES)
    return pl.pallas_call(
        _kpe_kernel,
        grid=(m // tm,),
        in_specs=[
            pl.BlockSpec((tm, d), lambda i: (i, 0)),
            pl.BlockSpec((d, LANE), lambda i: (0, 0)),
            pl.BlockSpec((tm, LANE), lambda i: (i, 0)),
            pl.BlockSpec((tm, LANE), lambda i: (i, 0)),
        ],
        out_specs=pl.BlockSpec((tm, LANE), lambda i: (i, 0)),
        out_shape=jax.ShapeDtypeStruct((m, LANE), F32),
        compiler_params=_cp("parallel"),
        name="kpe_proj",
    )(x, w_kpe, t1, t2)


MLA_QW = 2 * LANE


def _uq_kernel(cq_ref, w_ref, t1_ref, t2_ref, o_ref):
    acc = _dot(cq_ref[...].astype(BF16), w_ref[...])
    t1, t2 = t1_ref[...], t2_ref[...]
    for h in range(MLA_HEADS):
        c0 = h * MLA_QW
        o_ref[:, c0:c0 + LANE] = acc[:, c0:c0 + LANE].astype(BF16)
        x = acc[:, c0 + LANE:c0 + MLA_QW]
        o_ref[:, c0 + LANE:c0 + MLA_QW] = (x * t1 + _half_swap(x) * t2).astype(BF16)


def _uq_proj(proj, w_uq_ext, t1, t2):
    m = proj.shape[0]
    tm = _pick(m, ROW_TILES[:-1])
    n = MLA_HEADS * MLA_QW
    return pl.pallas_call(
        _uq_kernel,
        grid=(m // tm,),
        in_specs=[
            pl.BlockSpec((tm, MLA_RANK), lambda i: (i, EV_CQ_BLOCK)),
            pl.BlockSpec((MLA_RANK, n), lambda i: (0, 0)),
            pl.BlockSpec((tm, LANE), lambda i: (i, 0)),
            pl.BlockSpec((tm, LANE), lambda i: (i, 0)),
        ],
        out_specs=pl.BlockSpec((tm, n), lambda i: (i, 0)),
        out_shape=jax.ShapeDtypeStruct((m, n), BF16),
        compiler_params=_cp("parallel"),
        name="mla_q_up_proj",
    )(proj, w_uq_ext, t1, t2)


def _ukv_kernel(ckv_ref, w_ref, kpe_ref, k_ref, v_ref):
    acc = _dot(ckv_ref[...].astype(BF16), w_ref[...])
    kpe = kpe_ref[...].astype(BF16)
    for h in range(MLA_HEADS):
        c0 = h * MLA_QW
        k_ref[:, c0:c0 + LANE] = acc[:, c0:c0 + LANE].astype(BF16)
        k_ref[:, c0 + LANE:c0 + MLA_QW] = kpe
        v_ref[:, h * MLA_V:(h + 1) * MLA_V] = acc[:, c0 + LANE:c0 + MLA_QW].astype(BF16)


def _ukv_proj(proj, w_ukv, kpe_pad, mp):
    tm = _pick(mp, ROW_TILES[:-1])
    n = MLA_HEADS * MLA_QW
    return pl.pallas_call(
        _ukv_kernel,
        grid=(mp // tm,),
        in_specs=[
            pl.BlockSpec((tm, MLA_RANK), lambda i: (i, EV_CKV_BLOCK)),
            pl.BlockSpec((MLA_RANK, n), lambda i: (0, 0)),
            pl.BlockSpec((tm, LANE), lambda i: (i, 0)),
        ],
        out_specs=[pl.BlockSpec((tm, n), lambda i: (i, 0)),
                   pl.BlockSpec((tm, MLA_HEADS * MLA_V), lambda i: (i, 0))],
        out_shape=[jax.ShapeDtypeStruct((mp, n), BF16),
                   jax.ShapeDtypeStruct((mp, MLA_HEADS * MLA_V), BF16)],
        compiler_params=_cp("parallel"),
        name="mla_kv_up_proj",
    )(proj, w_ukv, kpe_pad)


def _ret_core(q, k, v, g, s0, dec, qd, kd, cd, gn_g, gn_b):
    qb, kb, vb = q.astype(BF16), k.astype(BF16), v.astype(BF16)
    s = _dot_nt(qb, kb) * dec
    o = _dot(s.astype(BF16), vb)
    o = o + _dot(qb, s0.astype(BF16)) * qd
    kk_t = (k * kd).T.astype(BF16)
    s_new = s0 * cd + _dot(kk_t, vb)
    on = _layer_norm(o, gn_g, gn_b)
    return on * (g * jax.nn.sigmoid(g)), s_new


def _ret_prompt_kernel(q_ref, k_ref, v_ref, g_ref, dec_ref, qd_ref, kd_ref, cd_ref, gg_ref, gb_ref,
                       o_ref, s_out_ref, s_ref, *, nc):
    c = pl.program_id(2)

    @pl.when(c == 0)
    def _():
        s_ref[...] = jnp.zeros_like(s_ref)

    for h in range(RET_HEAD_GROUP):
        cols = slice(h * RET_D, (h + 1) * RET_D)
        out, s_new = _ret_core(q_ref[:, cols], k_ref[:, cols], v_ref[:, cols], g_ref[:, cols], s_ref[h],
                               dec_ref[h], qd_ref[h], kd_ref[h], cd_ref[h], gg_ref[h], gb_ref[h])
        o_ref[:, cols] = out
        s_ref[h] = s_new

    @pl.when(c == nc - 1)
    def _():
        s_out_ref[...] = s_ref[...]


RET_HEAD_GROUP = 4


def _ret_prompt(proj, tabs, gn_g, gn_b, batch, seq):
    nc = seq // RET_CHUNK
    hg = RET_HEAD_GROUP
    gw = hg * RET_D
    groups = RET_HEADS // hg

    def tok(section):
        return pl.BlockSpec((RET_CHUNK, gw), lambda b, h, c: (b * nc + c, section * groups + h))

    def per_head(shape):
        return pl.BlockSpec((hg,) + shape, lambda b, h, c: (h, 0, 0))

    dec, qd, kd, cd = tabs
    return pl.pallas_call(
        functools.partial(_ret_prompt_kernel, nc=nc),
        grid=(batch, groups, nc),
        in_specs=[tok(0), tok(1), tok(2), tok(3),
                  per_head((RET_CHUNK, RET_CHUNK)), per_head((RET_CHUNK, LANE)), per_head((RET_CHUNK, LANE)),
                  per_head((1, LANE)), per_head((1, RET_D)), per_head((1, RET_D))],
        out_specs=[pl.BlockSpec((RET_CHUNK, gw), lambda b, h, c: (b * nc + c, h)),
                   pl.BlockSpec((None, hg, RET_D, RET_D), lambda b, h, c: (b, h, 0, 0))],
        out_shape=[jax.ShapeDtypeStruct((batch * seq, RET_HEADS * RET_D), F32),
                   jax.ShapeDtypeStruct((batch, RET_HEADS, RET_D, RET_D), F32)],
        scratch_shapes=[pltpu.VMEM((hg, RET_D, RET_D), F32)],
        compiler_params=_cp("parallel", "parallel", "arbitrary"),
        name="retention_prompt",
    )(proj, proj, proj, proj, dec, qd, kd, cd, gn_g[:, None, :], gn_b[:, None, :])


def _ret_sample_kernel(q_ref, k_ref, v_ref, g_ref, s0_ref, dec_ref, qd_ref, kd_ref, cd_ref, gg_ref, gb_ref,
                       o_ref, s_out_ref, *, ts):
    pad = jnp.zeros((RET_CHUNK - ts, RET_D), F32)

    def padded(ref, h):
        return jnp.concatenate([ref[:, h * RET_D:(h + 1) * RET_D], pad], axis=0)

    for h in range(RET_HEADS):
        out, s_new = _ret_core(padded(q_ref, h), padded(k_ref, h), padded(v_ref, h), padded(g_ref, h),
                               s0_ref[h], dec_ref[h], qd_ref[h], kd_ref[h], cd_ref[h], gg_ref[h], gb_ref[h])
        o_ref[:, h * RET_D:(h + 1) * RET_D] = out[:ts]
        s_out_ref[h] = s_new


def _ret_sample(proj, state, tabs, gn_g, gn_b, mp, bd, ts):
    width = RET_HEADS * RET_D
    row0 = mp // ts

    def tok(j):
        return pl.BlockSpec((ts, width), lambda b: (row0 + b, j))

    def const(a):
        return pl.BlockSpec(a.shape, lambda b: (0,) * a.ndim)

    dec, qd, kd, cd = tabs
    gg, gb = gn_g[:, None, :], gn_b[:, None, :]
    return pl.pallas_call(
        functools.partial(_ret_sample_kernel, ts=ts),
        grid=(bd,),
        in_specs=[tok(0), tok(1), tok(2), tok(3),
                  pl.BlockSpec((None, RET_HEADS, RET_D, RET_D), lambda b: (b, 0, 0, 0)),
                  const(dec), const(qd), const(kd), const(cd), const(gg), const(gb)],
        out_specs=[pl.BlockSpec((ts, width), lambda b: (b, 0)),
                   pl.BlockSpec((None, RET_HEADS, RET_D, RET_D), lambda b: (b, 0, 0, 0))],
        out_shape=[jax.ShapeDtypeStruct((bd * ts, width), F32),
                   jax.ShapeDtypeStruct(state.shape, F32)],
        compiler_params=_cp("parallel"),
        name="retention_sample",
    )(proj, proj, proj, proj, state, dec, qd, kd, cd, gg, gb)


def _ret_tables(chunk_len):
    h = jnp.arange(RET_HEADS, dtype=F32)
    lg = jnp.log(1.0 - jnp.exp2(-5.0 - h))
    idx = jnp.arange(RET_CHUNK, dtype=F32)
    diff = idx[:, None] - idx[None, :]
    dec = jnp.where(diff >= 0, jnp.exp(jnp.maximum(diff, 0.0)[None] * lg[:, None, None]), 0.0)
    qd = jnp.exp((idx + 1.0)[None, :] * lg[:, None])
    kd = jnp.exp((chunk_len - 1.0 - idx)[None, :] * lg[:, None])
    kd = jnp.where(idx[None, :] < chunk_len, kd, 0.0)
    cd = jnp.exp(chunk_len * lg)
    rep = lambda a: jnp.broadcast_to(a[..., None], a.shape + (LANE,))
    return dec, rep(qd), rep(kd), rep(cd[:, None])


def _mla_prompt_kernel(q_ref, k_ref, v_ref, o_ref, *, tq):
    i = pl.program_id(2)
    q = q_ref[...]

    def step(j, carry, masked):
        m, l, acc = carry
        rows = pl.ds(pl.multiple_of(j * tq, tq), tq)
        s = _dot_nt(q, k_ref[rows, :]) * MLA_SCALE
        if masked:
            r = lax.broadcasted_iota(jnp.int32, (tq, tq), 0)
            c = lax.broadcasted_iota(jnp.int32, (tq, tq), 1)
            s = jnp.where(c <= r, s, NEG)
        m_new = jnp.maximum(m, jnp.max(s, axis=1, keepdims=True))
        a = jnp.exp(m - m_new)
        p = jnp.exp(s - m_new)
        l = a * l + jnp.sum(p, axis=1, keepdims=True)
        acc = a * acc + _dot(p.astype(BF16), v_ref[rows, :])
        return m_new, l, acc

    init = (jnp.full((tq, 1), NEG, F32), jnp.zeros((tq, 1), F32), jnp.zeros((tq, MLA_V), F32))
    carry = lax.fori_loop(0, i, lambda j, c: step(j, c, False), init)
    m, l, acc = step(i, carry, True)
    o_ref[...] = acc / l


def _mla_prompt(q_cat, k_cat, v, batch, seq):
    tq = _pick(seq, (512, 256, 128))
    nq = seq // tq
    return pl.pallas_call(
        functools.partial(_mla_prompt_kernel, tq=tq),
        grid=(batch, MLA_HEADS, nq),
        in_specs=[pl.BlockSpec((tq, MLA_QW), lambda b, h, i: (b * nq + i, h)),
                  pl.BlockSpec((seq, MLA_QW), lambda b, h, i: (b, h)),
                  pl.BlockSpec((seq, MLA_V), lambda b, h, i: (b, h))],
        out_specs=pl.BlockSpec((tq, MLA_V), lambda b, h, i: (b * nq + i, h)),
        out_shape=jax.ShapeDtypeStruct((batch * seq, MLA_HEADS * MLA_V), F32),
        compiler_params=_cp("parallel", "parallel", "arbitrary"),
        name="mla_prompt_attention",
    )(q_cat, k_cat, v)


def _qlat_kernel(q_ref, w_ref, o_ref):
    o_ref[...] = _dot(q_ref[...], w_ref[...])


def _q_latent(q_cat, w_uk_t, mp, ms):
    tm = _pick(ms, ROW_TILES[:-1])
    row0 = mp // tm
    return pl.pallas_call(
        _qlat_kernel,
        grid=(MLA_HEADS, ms // tm),
        in_specs=[pl.BlockSpec((tm, MLA_NOPE), lambda h, i: (row0 + i, 2 * h)),
                  pl.BlockSpec((None, MLA_NOPE, MLA_RANK), lambda h, i: (h, 0, 0))],
        out_specs=pl.BlockSpec((None, tm, MLA_RANK), lambda h, i: (h, i, 0)),
        out_shape=jax.ShapeDtypeStruct((MLA_HEADS, ms, MLA_RANK), F32),
        compiler_params=_cp("parallel", "parallel"),
        name="mla_q_latent",
    )(q_cat, w_uk_t)


def _mla_sample_kernel(pt_ref, ql_ref, qp_ref, ckvn_ref, kpen_ref, ckv_hbm, kpe_hbm, o_ref,
                       ckv_buf, kpe_buf, kv_ref, kp_ref, sem, *, layer, npg, nchunk, ts):
    b = pl.program_id(0)
    nb = pl.num_programs(0)
    last = nb * nchunk - 1
    rows = MLA_HEADS * ts

    def chunk_copies(g, slot):
        cps = []
        for j in range(npg):
            page = pt_ref[g * npg + j]
            cps.append(pltpu.make_async_copy(ckv_hbm.at[layer, page], ckv_buf.at[slot, pl.ds(j * PAGE, PAGE)],
                                             sem.at[0, slot]))
            cps.append(pltpu.make_async_copy(kpe_hbm.at[layer, page], kpe_buf.at[slot, j], sem.at[1, slot]))
        return cps

    def ahead(g, k):
        return chunk_copies(jnp.minimum(g + k, last), lax.rem(g + k, MLA_SLOTS))

    @pl.when(b == 0)
    def _():
        for k in range(MLA_SLOTS - 1):
            for cp in ahead(0, k):
                cp.start()

    ql = ql_ref[...].reshape(rows, MLA_RANK).astype(BF16)
    qp = qp_ref[...].astype(BF16)

    def update(state, s, v):
        m, l, acc = state
        m_new = jnp.maximum(m, jnp.max(s, axis=1, keepdims=True))
        a = jnp.exp(m - m_new)
        p = jnp.exp(s - m_new)
        return m_new, a * l + jnp.sum(p, axis=1, keepdims=True), a * acc + _dot(p.astype(BF16), v)

    state = (jnp.full((rows, 1), NEG, F32), jnp.zeros((rows, 1), F32), jnp.zeros((rows, MLA_RANK), F32))
    for c in range(nchunk):
        g = b * nchunk + c
        for cp in ahead(g, MLA_SLOTS - 1):
            cp.start()
        slot = lax.rem(g, MLA_SLOTS)
        for cp in chunk_copies(g, slot):
            cp.wait()
        kvb, kpb = kv_ref.at[c % 2], kp_ref.at[c % 2]
        kvb[...] = ckv_buf[slot].astype(BF16)
        for j in range(npg):
            kpb[:, j * PAGE:(j + 1) * PAGE] = kpe_buf[slot, j].astype(BF16)
        kv = kvb[...]
        state = update(state, (_dot_nt(ql, kv) + _dot(qp, kpb[...])) * MLA_SCALE, kv)

    ckv = jnp.concatenate([ckvn_ref[...], jnp.zeros((PAGE - ts, MLA_RANK), F32)], axis=0).astype(BF16)
    kpe = jnp.concatenate([kpen_ref[...], jnp.zeros((PAGE - ts, MLA_ROPE), F32)], axis=0).astype(BF16)
    s = (_dot_nt(ql, ckv) + _dot_nt(qp, kpe)) * MLA_SCALE
    t = lax.broadcasted_iota(jnp.int32, (rows, PAGE), 0) % ts
    key = lax.broadcasted_iota(jnp.int32, (rows, PAGE), 1)
    _, l, acc = update(state, jnp.where(key <= t, s, NEG), ckv)
    o_ref[...] = acc / l

    @pl.when(b == nb - 1)
    def _():
        for k in range(1, MLA_SLOTS):
            for cp in ahead(last, k):
                cp.wait()


MLA_CHUNK_PAGES = (16, 8, 4, 2, 1)
MLA_SLOTS = 3


def _mla_sample(page_table_flat, q_lat, q_pe, ckv_new, kpe_new, cache_ckv, cache_kpe_t, layer, bd, ts, n_pages):
    npg = _pick(n_pages, MLA_CHUNK_PAGES)
    nchunk = n_pages // npg
    rows = MLA_HEADS * ts
    in_specs = [pl.BlockSpec((MLA_HEADS, ts, MLA_RANK), lambda b, pt: (0, b, 0)),
                pl.BlockSpec((None, rows, MLA_ROPE), lambda b, pt: (b, 0, 0)),
                pl.BlockSpec((None, ts, MLA_RANK), lambda b, pt: (b, 0, 0)),
                pl.BlockSpec((None, ts, MLA_ROPE), lambda b, pt: (b, 0, 0)),
                pl.BlockSpec(memory_space=pl.ANY),
                pl.BlockSpec(memory_space=pl.ANY)]
    return pl.pallas_call(
        functools.partial(_mla_sample_kernel, layer=layer, npg=npg, nchunk=nchunk, ts=ts),
        grid_spec=pltpu.PrefetchScalarGridSpec(
            num_scalar_prefetch=1,
            grid=(bd,),
            in_specs=in_specs,
            out_specs=pl.BlockSpec((None, rows, MLA_RANK), lambda b, pt: (b, 0, 0)),
            scratch_shapes=[pltpu.VMEM((MLA_SLOTS, npg * PAGE, MLA_RANK), F32),
                            pltpu.VMEM((MLA_SLOTS, npg, MLA_ROPE, PAGE), F32),
                            pltpu.VMEM((2, npg * PAGE, MLA_RANK), BF16), pltpu.VMEM((2, MLA_ROPE, npg * PAGE), BF16),
                            pltpu.SemaphoreType.DMA((2, MLA_SLOTS))],
        ),
        out_shape=jax.ShapeDtypeStruct((bd, rows, MLA_RANK), F32),
        compiler_params=_cp("arbitrary"),
        name="mla_sample_attention",
    )(page_table_flat, q_lat, q_pe, ckv_new, kpe_new, cache_ckv, cache_kpe_t)


def _uv_kernel(o_ref, w_ref, out_ref, *, ts):
    tb = o_ref.shape[0]
    o = o_ref[...].reshape(tb * ts, MLA_RANK).astype(BF16)
    out_ref[...] = _dot(o, w_ref[...])


def _mla_value_up(o_lat, w_uv, bd, ts):
    tb = _pick(bd, (64, 32, 16, 8, 4, 2, 1))
    return pl.pallas_call(
        functools.partial(_uv_kernel, ts=ts),
        grid=(MLA_HEADS, bd // tb),
        in_specs=[pl.BlockSpec((tb, None, ts, MLA_RANK), lambda h, i: (i, h, 0, 0)),
                  pl.BlockSpec((None, MLA_RANK, MLA_V), lambda h, i: (h, 0, 0))],
        out_specs=pl.BlockSpec((tb * ts, MLA_V), lambda h, i: (i, h)),
        out_shape=jax.ShapeDtypeStruct((bd * ts, MLA_HEADS * MLA_V), F32),
        compiler_params=_cp("parallel", "parallel"),
        name="mla_value_up_proj",
    )(o_lat, w_uv)


def _mm_kernel(x_ref, w_ref, o_ref, xb_ref):
    @pl.when(pl.program_id(1) == 0)
    def _():
        xb_ref[...] = x_ref[...].astype(BF16)

    o_ref[...] = _dot(xb_ref[...], w_ref[...])


def _matmul(x, w):
    m, d = x.shape
    n = w.shape[1]
    tm = _pick(m, PROJ_ROW_TILES)
    tn = _pick(n, (512, 256, 128))
    return pl.pallas_call(
        _mm_kernel,
        grid=(m // tm, n // tn),
        in_specs=[pl.BlockSpec((tm, d), lambda i, j: (i, 0)),
                  pl.BlockSpec((d, tn), lambda i, j: (0, j))],
        out_specs=pl.BlockSpec((tm, tn), lambda i, j: (i, j)),
        out_shape=jax.ShapeDtypeStruct((m, n), F32),
        scratch_shapes=[pltpu.VMEM((tm, d), BF16)],
        compiler_params=_cp("parallel", "arbitrary"),
        name="sb_in_proj",
    )(x, w)


SB_Q_COLS = SB_HEADS * SB_DIM
SB_KV_COLS = SB_KV_HEADS * SB_DIM
SB_TK = 128
SB_EXIT = -104.0


def _sb_block(z, mask, carry, tri_ones):
    r = z.shape[0]
    sp = jnp.maximum(z, 0.0) + jnp.log1p(jnp.exp(-jnp.abs(z)))
    if mask is not None:
        sp_m = jnp.where(mask, sp, 0.0)
    else:
        sp_m = sp
    hi = sp_m.astype(BF16)
    lo = (sp_m - hi.astype(F32)).astype(BF16)
    cs = _dot(jnp.concatenate([hi, lo], axis=0), tri_ones)
    cs = cs[:r] + cs[r:]
    a = jnp.exp(z - sp + carry - cs[:, :SB_TK])
    if mask is not None:
        a = jnp.where(mask, a, 0.0)
    return a, carry - cs[:, SB_TK:]


def _sb_live(carry):
    return (jnp.max(carry) > SB_EXIT).astype(jnp.int32)


def _sb_prompt_kernel(q_ref, k_ref, v_ref, to_ref, o_ref, *, tq):
    i = pl.program_id(2)
    rows = SB_GROUP * tq
    q = jnp.concatenate([q_ref[:, g * SB_DIM:(g + 1) * SB_DIM] for g in range(SB_GROUP)], axis=0).astype(BF16)
    tri_ones = to_ref[...]

    def step(j, carry, acc, mask):
        ks = pl.ds(pl.multiple_of(j * SB_TK, SB_TK), SB_TK)
        z = _dot_nt(q, k_ref[ks, :].astype(BF16)) * SB_SCALE
        a, carry = _sb_block(z, mask, carry, tri_ones)
        return carry, acc + _dot(a.astype(BF16), v_ref[ks, :].astype(BF16))

    t = lax.broadcasted_iota(jnp.int32, (rows, SB_TK), 0) % tq
    key = lax.broadcasted_iota(jnp.int32, (rows, SB_TK), 1)
    carry, acc = step(i, jnp.zeros((rows, SB_TK), F32), jnp.zeros((rows, SB_DIM), F32), key < t)

    def body(state):
        j, _, carry, acc = state
        carry, acc = step(j, carry, acc, None)
        return j - 1, _sb_live(carry), carry, acc

    _, _, carry, acc = lax.while_loop(lambda s: (s[0] >= 0) & (s[1] > 0), body,
                                      (i - 1, _sb_live(carry), carry, acc))
    for g in range(SB_GROUP):
        o_ref[:, g * SB_DIM:(g + 1) * SB_DIM] = acc[g * tq:(g + 1) * tq]


def _sb_prompt(proj, tri_ones, batch, seq):
    tq = SB_TK
    nq = seq // tq
    gw = SB_GROUP * SB_DIM
    kb0 = SB_Q_COLS // SB_DIM
    vb0 = kb0 + SB_KV_HEADS
    return pl.pallas_call(
        functools.partial(_sb_prompt_kernel, tq=tq),
        grid=(batch, SB_KV_HEADS, nq),
        in_specs=[pl.BlockSpec((tq, gw), lambda b, h, i: (b * nq + i, h)),
                  pl.BlockSpec((seq, SB_DIM), lambda b, h, i: (b, kb0 + h)),
                  pl.BlockSpec((seq, SB_DIM), lambda b, h, i: (b, vb0 + h)),
                  pl.BlockSpec((SB_TK, 2 * SB_TK), lambda b, h, i: (0, 0))],
        out_specs=pl.BlockSpec((tq, gw), lambda b, h, i: (b * nq + i, h)),
        out_shape=jax.ShapeDtypeStruct((batch * seq, SB_Q_COLS), F32),
        compiler_params=_cp("parallel", "parallel", "arbitrary"),
        name="sb_prompt_attention",
    )(proj, proj, proj, tri_ones)


def _sb_sample_kernel(pt_ref, q_ref, kn_ref, vn_ref, to_ref, kc_ref, vc_ref, o_ref,
                      kfirst, vfirst, kmore, vmore, sem_first, sem_more, *, ts, npg, nchunk):
    b = pl.program_id(0)
    nb = pl.num_programs(0)
    n_pages = npg * nchunk
    page_rows = PAGE * SB_KV_HEADS
    hrows = SB_GROUP * ts
    rows = SB_KV_HEADS * hrows
    q = jnp.concatenate([q_ref[:, h * SB_DIM:(h + 1) * SB_DIM] for h in range(SB_HEADS)], axis=0).astype(BF16)
    tri_ones = to_ref[...]

    def chunk_copies(seq, c, kbuf, vbuf, ksem, vsem):
        cps = []
        for j in range(npg):
            page = pt_ref[seq * n_pages + n_pages - 1 - (c * npg + j)]
            rows_j = pl.ds(j * page_rows, page_rows)
            cps.append(pltpu.make_async_copy(kc_ref.at[page], kbuf.at[rows_j], ksem))
            cps.append(pltpu.make_async_copy(vc_ref.at[page], vbuf.at[rows_j], vsem))
        return cps

    def first_copies(seq, slot):
        return chunk_copies(seq, 0, kfirst.at[slot], vfirst.at[slot], sem_first.at[0, slot], sem_first.at[1, slot])

    slot = lax.rem(b, 2)

    @pl.when(b == 0)
    def _():
        for cp in first_copies(b, slot):
            cp.start()

    nxt = first_copies(jnp.minimum(b + 1, nb - 1), 1 - slot)
    for cp in nxt:
        cp.start()

    def block(k_of, v_of, mask, carry, acc):
        z = jnp.concatenate(
            [_dot_nt(q[h * hrows:(h + 1) * hrows], k_of(h).astype(BF16)) for h in range(SB_KV_HEADS)],
            axis=0) * SB_SCALE
        a, carry = _sb_block(z, mask, carry, tri_ones)
        ab = a.astype(BF16)
        upd = jnp.concatenate(
            [_dot(ab[h * hrows:(h + 1) * hrows], v_of(h).astype(BF16)) for h in range(SB_KV_HEADS)],
            axis=0)
        return carry, acc + upd

    zpad = jnp.zeros((PAGE - ts, SB_KV_COLS), F32)
    kn = jnp.concatenate([kn_ref[...], zpad], axis=0)
    vn = jnp.concatenate([vn_ref[...], zpad], axis=0)
    t = lax.broadcasted_iota(jnp.int32, (rows, SB_TK), 0) % ts
    key = lax.broadcasted_iota(jnp.int32, (rows, SB_TK), 1)
    carry, acc = block(lambda h: kn[:, h * SB_DIM:(h + 1) * SB_DIM], lambda h: vn[:, h * SB_DIM:(h + 1) * SB_DIM],
                       key < t, jnp.zeros((rows, SB_TK), F32), jnp.zeros((rows, SB_DIM), F32))

    def chunk_blocks(k_rows, v_rows, carry, acc):
        for j in range(npg):
            def head(read, h, j=j):
                return read(pl.ds(j * page_rows + h, PAGE, stride=SB_KV_HEADS))
            carry, acc = block(lambda h: head(k_rows, h), lambda h: head(v_rows, h), None, carry, acc)
        return carry, acc

    for cp in first_copies(b, slot):
        cp.wait()
    carry, acc = chunk_blocks(lambda r: kfirst[slot, r, :], lambda r: vfirst[slot, r, :], carry, acc)

    def body(state):
        c, _, carry, acc = state
        cps = chunk_copies(b, c, kmore, vmore, sem_more.at[0], sem_more.at[1])
        for cp in cps:
            cp.start()
        for cp in cps:
            cp.wait()
        carry, acc = chunk_blocks(lambda r: kmore[r, :], lambda r: vmore[r, :], carry, acc)
        return c + 1, _sb_live(carry), carry, acc

    _, _, carry, acc = lax.while_loop(lambda s: (s[0] < nchunk) & (s[1] > 0), body,
                                      (jnp.int32(1), _sb_live(carry), carry, acc))

    for h in range(SB_HEADS):
        o_ref[:, h * SB_DIM:(h + 1) * SB_DIM] = acc[h * ts:(h + 1) * ts]

    @pl.when(b == nb - 1)
    def _():
        for cp in nxt:
            cp.wait()


SB_CHUNK_PAGES = (2, 1)


def _sb_sample(page_table_flat, proj, tri_ones, cache_k, cache_v, mp, bd, ts, n_pages):
    row0 = mp // ts
    kb0 = SB_Q_COLS // SB_KV_COLS
    npg = _pick(n_pages, SB_CHUNK_PAGES)
    chunk_rows = npg * PAGE * SB_KV_HEADS
    in_specs = [pl.BlockSpec((ts, SB_Q_COLS), lambda b, pt: (row0 + b, 0)),
                pl.BlockSpec((ts, SB_KV_COLS), lambda b, pt: (row0 + b, kb0)),
                pl.BlockSpec((ts, SB_KV_COLS), lambda b, pt: (row0 + b, kb0 + 1)),
                pl.BlockSpec((SB_TK, 2 * SB_TK), lambda b, pt: (0, 0)),
                pl.BlockSpec(memory_space=pl.ANY),
                pl.BlockSpec(memory_space=pl.ANY)]
    return pl.pallas_call(
        functools.partial(_sb_sample_kernel, ts=ts, npg=npg, nchunk=n_pages // npg),
        grid_spec=pltpu.PrefetchScalarGridSpec(
            num_scalar_prefetch=1,
            grid=(bd,),
            in_specs=in_specs,
            out_specs=pl.BlockSpec((ts, SB_Q_COLS), lambda b, pt: (b, 0)),
            scratch_shapes=[pltpu.VMEM((2, chunk_rows, SB_DIM), F32), pltpu.VMEM((2, chunk_rows, SB_DIM), F32),
                            pltpu.VMEM((chunk_rows, SB_DIM), F32), pltpu.VMEM((chunk_rows, SB_DIM), F32),
                            pltpu.SemaphoreType.DMA((2, 2)), pltpu.SemaphoreType.DMA((2,))],
        ),
        out_shape=jax.ShapeDtypeStruct((bd * ts, SB_Q_COLS), F32),
        compiler_params=_cp("arbitrary"),
        name="sb_sample_attention",
    )(page_table_flat, proj, proj, proj, tri_ones, cache_k, cache_v)


def _rope_tables(pos):
    p = pos.astype(F32)[:, None]
    inv_r = 1.0 / (ROPE_BASE ** jnp.linspace(0.0, 1.0, RET_D // 2, dtype=F32))
    ang = p * inv_r[None, :]
    c, s = jnp.cos(ang), jnp.sin(ang)
    cr = jnp.concatenate([c, c], axis=1)
    sr = jnp.concatenate([-s, s], axis=1)
    inv_m = 1.0 / (ROPE_BASE ** (jnp.arange(0, MLA_ROPE, 2, dtype=F32) / MLA_ROPE))
    ang = p * inv_m[None, :]
    c, s = jnp.cos(ang), jnp.sin(ang)
    z = jnp.zeros((pos.shape[0], LANE - MLA_ROPE), F32)
    t1 = jnp.concatenate([c, c, z], axis=1)
    t2 = jnp.concatenate([-s, s, z], axis=1)
    return cr, sr, t1, t2


def _swap_halves(w):
    half = w.shape[-1] // 2
    return jnp.concatenate([w[..., half:], w[..., :half]], axis=-1)


def kernel(x_prompt, x_sample, state_ret, cache_mla_ckv, cache_mla_kpe, cache_sb_k, cache_sb_v, page_table,
           ln_g, ln_b, ffn_w_gate, ffn_w_up, ffn_w_down,
           ev_w_in, ev_q_norm, ev_w_uq, ev_kv_norm, ev_w_ukv, ev_gn_g, ev_gn_b, ev_w_o,
           od_w_in, od_w_o):
    batch, seq, d = x_prompt.shape
    bd, ts, _ = x_sample.shape
    n_pages = page_table.shape[1]
    past = n_pages * PAGE
    mp, ms = batch * seq, bd * ts
    assert seq % RET_CHUNK == 0 and ts <= 8 and mp % ts == 0

    x = jnp.concatenate([x_prompt.reshape(mp, d), x_sample.reshape(ms, d)], axis=0)
    pos = jnp.concatenate([jnp.tile(jnp.arange(seq), batch), jnp.tile(past + jnp.arange(ts), bd)])
    cr, sr, t1, t2 = _rope_tables(pos)
    pt_flat = page_table.reshape(-1)
    tri = (jnp.arange(SB_TK)[:, None] > jnp.arange(SB_TK)[None, :]).astype(BF16)
    tri_ones = jnp.concatenate([tri, jnp.ones((SB_TK, SB_TK), BF16)], axis=1)
    tabs_p = _ret_tables(float(RET_CHUNK))
    tabs_s = _ret_tables(float(ts))

    wg, wu, wd = ffn_w_gate.astype(BF16), ffn_w_up.astype(BF16), ffn_w_down.astype(BF16)

    outs = {k: [] for k in ("ret_p", "ret_s", "ckv_p", "ckv_s", "kpe_p", "kpe_s", "sbk_p", "sbk_s", "sbv_p", "sbv_s")}
    for l in range(DEPTH):
        x = _ffn_half_step(x, wg, wu, wd, l, 0, ln_g[l, 0], ln_b[l, 0])
        if l % 2 == 0:
            e = l // 2
            w_in = ev_w_in[e]
            w_main = w_in[:, :EV_MAIN_COLS].astype(BF16)
            w_kpe = w_in[:, EV_MAIN_COLS:]
            w_kpe = jnp.concatenate([w_kpe, _swap_halves(w_kpe)], axis=1).astype(BF16)
            proj = _ev_proj(x, w_main, cr, sr, ev_q_norm[e], ev_kv_norm[e])
            kpe_pad = _kpe_proj(x, w_kpe, t1, t2)

            w_uq = ev_w_uq[e].reshape(MLA_RANK, MLA_HEADS, MLA_NOPE + MLA_ROPE)
            w_uq_ext = jnp.concatenate([w_uq, _swap_halves(w_uq[..., MLA_NOPE:])], axis=-1)
            q_cat = _uq_proj(proj, w_uq_ext.reshape(MLA_RANK, MLA_HEADS * MLA_QW).astype(BF16), t1, t2)

            w_ukv = ev_w_ukv[e]
            k_cat, v_p = _ukv_proj(proj, w_ukv.astype(BF16), kpe_pad, mp)

            ret_out_p, s_p = _ret_prompt(proj, tabs_p, ev_gn_g[e], ev_gn_b[e], batch, seq)
            ret_out_s, s_s = _ret_sample(proj, state_ret[e], tabs_s, ev_gn_g[e], ev_gn_b[e], mp, bd, ts)
            mla_out_p = _mla_prompt(q_cat, k_cat, v_p, batch, seq)

            w3 = w_ukv.reshape(MLA_RANK, MLA_HEADS, MLA_NOPE + MLA_V)
            w_uk_t = w3[..., :MLA_NOPE].transpose(1, 2, 0).astype(BF16)
            w_uv = w3[..., MLA_NOPE:].transpose(1, 0, 2).astype(BF16)
            q_lat = _q_latent(q_cat, w_uk_t, mp, ms)
            q_pe_s = q_cat[mp:].reshape(bd, ts, MLA_HEADS, MLA_QW)[..., MLA_NOPE:MLA_NOPE + MLA_ROPE]
            q_pe_s = q_pe_s.transpose(0, 2, 1, 3).reshape(bd, MLA_HEADS * ts, MLA_ROPE)
            ckv = proj[:, EV_CKV_BLOCK * EV_BLOCK:(EV_CKV_BLOCK + 1) * EV_BLOCK]
            kpe = kpe_pad[:, :MLA_ROPE]
            ckv_s = ckv[mp:].reshape(bd, ts, MLA_RANK)
            kpe_s = kpe[mp:].reshape(bd, ts, MLA_ROPE)
            o_lat = _mla_sample(pt_flat, q_lat, q_pe_s, ckv_s, kpe_s, cache_mla_ckv,
                                jnp.swapaxes(cache_mla_kpe, 2, 3), e, bd, ts, n_pages)
            mla_out_s = _mla_value_up(o_lat.reshape(bd, MLA_HEADS, ts, MLA_RANK), w_uv, bd, ts)

            w_o = ev_w_o[e].astype(BF16)
            n_ret = RET_HEADS * RET_D
            x = _out_proj_ln([jnp.concatenate([ret_out_p, ret_out_s], axis=0),
                              jnp.concatenate([mla_out_p, mla_out_s], axis=0)],
                             [w_o[:n_ret], w_o[n_ret:]], x, ln_g[l, 1], ln_b[l, 1])
            outs["ret_p"].append(s_p)
            outs["ret_s"].append(s_s)
            outs["ckv_p"].append(ckv[:mp].reshape(batch, seq, MLA_RANK))
            outs["ckv_s"].append(ckv_s)
            outs["kpe_p"].append(kpe[:mp].reshape(batch, seq, MLA_ROPE))
            outs["kpe_s"].append(kpe_s)
        else:
            o = l // 2
            proj = _matmul(x, od_w_in[o].astype(BF16))
            n_kv = cache_sb_k.shape[1]
            cache_k = cache_sb_k[o].reshape(n_kv, PAGE * SB_KV_HEADS, SB_DIM)
            cache_v = cache_sb_v[o].reshape(n_kv, PAGE * SB_KV_HEADS, SB_DIM)
            att_p = _sb_prompt(proj, tri_ones, batch, seq)
            att_s = _sb_sample(pt_flat, proj, tri_ones, cache_k, cache_v, mp, bd, ts, n_pages)
            x = _out_proj_ln([jnp.concatenate([att_p, att_s], axis=0)], [od_w_o[o].astype(BF16)],
                             x, ln_g[l, 1], ln_b[l, 1])
            k = proj[:, SB_Q_COLS:SB_Q_COLS + SB_KV_COLS]
            v = proj[:, SB_Q_COLS + SB_KV_COLS:]
            outs["sbk_p"].append(k[:mp].reshape(batch, seq, SB_KV_HEADS, SB_DIM))
            outs["sbk_s"].append(k[mp:].reshape(bd, ts, SB_KV_HEADS, SB_DIM))
            outs["sbv_p"].append(v[:mp].reshape(batch, seq, SB_KV_HEADS, SB_DIM))
            outs["sbv_s"].append(v[mp:].reshape(bd, ts, SB_KV_HEADS, SB_DIM))
        x = _ffn_half_step(x, wg, wu, wd, l, 1, ln_g[l, 2], ln_b[l, 2])

    y_p = x[:mp].reshape(batch, seq, d)
    y_s = x[mp:].reshape(bd, ts, d)
    return (y_p, y_s, jnp.stack(outs["ret_p"]), jnp.stack(outs["ret_s"]),
            jnp.stack(outs["ckv_p"]), jnp.stack(outs["ckv_s"]), jnp.stack(outs["kpe_p"]), jnp.stack(outs["kpe_s"]),
            jnp.stack(outs["sbk_p"]), jnp.stack(outs["sbk_s"]), jnp.stack(outs["sbv_p"]), jnp.stack(outs["sbv_s"]))
```

```python
import functools

import jax
import jax.numpy as jnp
import numpy as np
from jax import lax
from jax.experimental import pallas as pl
from jax.experimental.pallas import tpu as pltpu

F32 = jnp.float32
BF16 = jnp.bfloat16

DEPTH = 2
PAGE = 128
RET_HEADS = 8
RET_D = 128
MLA_HEADS = 8
MLA_NOPE = 128
MLA_ROPE = 64
MLA_V = 128
MLA_RANK = 512
MLA_SCALE = (MLA_NOPE + MLA_ROPE) ** -0.5
SB_HEADS = 16
SB_KV_HEADS = 4
SB_GROUP = SB_HEADS // SB_KV_HEADS
SB_DIM = 128
SB_SCALE = SB_DIM ** -0.5
ROPE_BASE = 10000.0
EPS = 1e-5
ALPHA = (2 * DEPTH) ** 0.25
RET_CHUNK = 128
NEG = -1e30

LANE = 128
VMEM_LIMIT = 56 * 1024 * 1024
ROW_TILES = (512, 256, 128, 64, 32, 16, 8)
FFN_ROW_TILES = (768,) + ROW_TILES
PROJ_ROW_TILES = (1024,) + ROW_TILES


def _pick(n, cands):
    for c in cands:
        if n % c == 0:
            return c
    raise ValueError(f"no tile for {n}")


def _cp(*sem):
    return pltpu.CompilerParams(dimension_semantics=sem, vmem_limit_bytes=VMEM_LIMIT)


def _dot(a, b):
    return jnp.dot(a, b, preferred_element_type=F32)


def _dot_nt(a, b):
    return lax.dot_general(a, b, (((1,), (1,)), ((), ())), preferred_element_type=F32)


def _layer_norm(y, g, b):
    mu = jnp.mean(y, axis=-1, keepdims=True)
    d = y - mu
    var = jnp.mean(d * d, axis=-1, keepdims=True)
    return d * lax.rsqrt(var + EPS) * g + b


def _rms_norm(y, g):
    return y * lax.rsqrt(jnp.mean(y * y, axis=-1, keepdims=True) + EPS) * g


def _half_swap(x):
    return pltpu.roll(x, LANE // 2, 1)


def _ffn_kernel(x_ref, wg_ref, wu_ref, wd_ref, g_ref, b_ref, o_ref, xb_ref, acc_ref, *, nf):
    f = pl.program_id(1)

    @pl.when(f == 0)
    def _():
        xb_ref[...] = x_ref[...].astype(BF16)
        acc_ref[...] = jnp.zeros_like(acc_ref)

    xb = xb_ref[...]
    g = _dot(xb, wg_ref[...])
    u = _dot(xb, wu_ref[...])
    a = (g * jax.nn.sigmoid(g) * u).astype(BF16)
    acc_ref[...] += _dot(a, wd_ref[...])

    @pl.when(f == nf - 1)
    def _():
        y = ALPHA * x_ref[...] + 0.5 * acc_ref[...]
        o_ref[...] = _layer_norm(y, g_ref[...], b_ref[...])


def _ffn_half_step(x, wg, wu, wd, layer, half, g, b):
    m, d = x.shape
    ff = wg.shape[-1]
    tm = _pick(m, FFN_ROW_TILES)
    tf = _pick(ff, (512, 256, 128))
    nf = ff // tf
    return pl.pallas_call(
        functools.partial(_ffn_kernel, nf=nf),
        grid=(m // tm, nf),
        in_specs=[
            pl.BlockSpec((tm, d), lambda i, f: (i, 0)),
            pl.BlockSpec((None, None, d, tf), lambda i, f: (layer, half, 0, f)),
            pl.BlockSpec((None, None, d, tf), lambda i, f: (layer, half, 0, f)),
            pl.BlockSpec((None, None, tf, d), lambda i, f: (layer, half, f, 0)),
            pl.BlockSpec((1, d), lambda i, f: (0, 0)),
            pl.BlockSpec((1, d), lambda i, f: (0, 0)),
        ],
        out_specs=pl.BlockSpec((tm, d), lambda i, f: (i, 0)),
        out_shape=jax.ShapeDtypeStruct((m, d), F32),
        scratch_shapes=[pltpu.VMEM((tm, d), BF16), pltpu.VMEM((tm, d), F32)],
        compiler_params=_cp("parallel", "arbitrary"),
        name="ffn_half_step",
    )(x, wg, wu, wd, g.reshape(1, d), b.reshape(1, d))


def _out_ln_kernel(*refs, n_parts, prompt_blocks):
    p_refs = refs[:n_parts]
    s_refs = refs[n_parts:2 * n_parts]
    w_refs = refs[2 * n_parts:3 * n_parts]
    x_ref, g_ref, b_ref, o_ref = refs[3 * n_parts:]
    is_prompt = pl.program_id(0) < prompt_blocks
    acc = None
    for p_ref, s_ref, w_ref in zip(p_refs, s_refs, w_refs):
        a = jnp.where(is_prompt, p_ref[...], s_ref[...]).astype(BF16)
        part = _dot(a, w_ref[...])
        acc = part if acc is None else acc + part
    y = ALPHA * x_ref[...] + acc
    o_ref[...] = _layer_norm(y, g_ref[...], b_ref[...])


def _out_proj_ln(parts, weights, x, g, b):
    m, d = x.shape
    mp, ms = parts[0][0].shape[0], parts[0][1].shape[0]
    tm = _pick(int(np.gcd(mp, ms)), ROW_TILES)
    npb = mp // tm
    n = len(parts)
    in_specs = [pl.BlockSpec((tm, p.shape[1]), lambda i: (jnp.minimum(i, npb - 1), 0)) for p, _ in parts]
    in_specs += [pl.BlockSpec((tm, s.shape[1]), lambda i: (jnp.maximum(i - npb, 0), 0)) for _, s in parts]
    in_specs += [pl.BlockSpec(w.shape, lambda i: (0, 0)) for w in weights]
    in_specs += [pl.BlockSpec((tm, d), lambda i: (i, 0)),
                 pl.BlockSpec((1, d), lambda i: (0, 0)),
                 pl.BlockSpec((1, d), lambda i: (0, 0))]
    return pl.pallas_call(
        functools.partial(_out_ln_kernel, n_parts=n, prompt_blocks=npb),
        grid=(m // tm,),
        in_specs=in_specs,
        out_specs=pl.BlockSpec((tm, d), lambda i: (i, 0)),
        out_shape=jax.ShapeDtypeStruct((m, d), F32),
        compiler_params=_cp("parallel"),
        name="out_proj_ln",
    )(*[p for p, _ in parts], *[s for _, s in parts], *weights, x, g.reshape(1, d), b.reshape(1, d))


EV_BLOCK = 512
EV_Q_BLOCKS = (0, 2)
EV_K_BLOCKS = (2, 4)
EV_CQ_BLOCK = 8
EV_CKV_BLOCK = 9
EV_MAIN_COLS = 10 * EV_BLOCK


def _ev_proj_kernel(x_ref, w_ref, cr_ref, sr_ref, gq_ref, gkv_ref, o_ref, xb_ref):
    j = pl.program_id(1)

    @pl.when(j == 0)
    def _():
        xb_ref[...] = x_ref[...].astype(BF16)

    acc = _dot(xb_ref[...], w_ref[...])

    def rope(scale):
        c, s = cr_ref[...], sr_ref[...]
        for h in range(EV_BLOCK // LANE):
            blk = acc[:, h * LANE:(h + 1) * LANE]
            r = blk * c + _half_swap(blk) * s
            o_ref[:, h * LANE:(h + 1) * LANE] = r if scale is None else r * scale

    @pl.when(j < EV_Q_BLOCKS[1])
    def _():
        rope(None)

    @pl.when((j >= EV_K_BLOCKS[0]) & (j < EV_K_BLOCKS[1]))
    def _():
        rope(RET_D ** -0.5)

    @pl.when((j >= EV_K_BLOCKS[1]) & (j < EV_CQ_BLOCK))
    def _():
        o_ref[...] = acc

    @pl.when(j == EV_CQ_BLOCK)
    def _():
        o_ref[...] = _rms_norm(acc, gq_ref[...])

    @pl.when(j == EV_CKV_BLOCK)
    def _():
        o_ref[...] = _rms_norm(acc, gkv_ref[...])


def _ev_proj(x, w_main, cr, sr, gq, gkv):
    m, d = x.shape
    tm = _pick(m, PROJ_ROW_TILES)
    nb = EV_MAIN_COLS // EV_BLOCK
    return pl.pallas_call(
        _ev_proj_kernel,
        grid=(m // tm, nb),
        in_specs=[
            pl.BlockSpec((tm, d), lambda i, j: (i, 0)),
            pl.BlockSpec((d, EV_BLOCK), lambda i, j: (0, j)),
            pl.BlockSpec((tm, LANE), lambda i, j: (i, 0)),
            pl.BlockSpec((tm, LANE), lambda i, j: (i, 0)),
            pl.BlockSpec((1, MLA_RANK), lambda i, j: (0, 0)),
            pl.BlockSpec((1, MLA_RANK), lambda i, j: (0, 0)),
        ],
        out_specs=pl.BlockSpec((tm, EV_BLOCK), lambda i, j: (i, j)),
        out_shape=jax.ShapeDtypeStruct((m, EV_MAIN_COLS), F32),
        scratch_shapes=[pltpu.VMEM((tm, d), BF16)],
        compiler_params=_cp("parallel", "arbitrary"),
        name="even_in_proj",
    )(x, w_main, cr, sr, gq.reshape(1, -1), gkv.reshape(1, -1))


def _kpe_kernel(x_ref, w_ref, t1_ref, t2_ref, o_ref):
    acc = _dot(x_ref[...].astype(BF16), w_ref[...])
    o_ref[...] = acc * t1_ref[...] + _half_swap(acc) * t2_ref[...]


def _kpe_proj(x, w_kpe, t1, t2):
    m, d = x.shape
    tm = _pick(m, ROW_TILES)
    return pl.pallas_call(
        _kpe_kernel,
        grid=(m // tm,),
        in_specs=[
            pl.BlockSpec((tm, d), lambda i: (i, 0)),
            pl.BlockSpec((d, LANE), lambda i: (0, 0)),
            pl.BlockSpec((tm, LANE), lambda i: (i, 0)),
            pl.BlockSpec((tm, LANE), lambda i: (i, 0)),
        ],
        out_specs=pl.BlockSpec((tm, LANE), lambda i: (i, 0)),
        out_shape=jax.ShapeDtypeStruct((m, LANE), F32),
        compiler_params=_cp("parallel"),
        name="kpe_proj",
    )(x, w_kpe, t1, t2)


MLA_QW = 2 * LANE


def _uq_kernel(cq_ref, w_ref, t1_ref, t2_ref, o_ref):
    acc = _dot(cq_ref[...].astype(BF16), w_ref[...])
    t1, t2 = t1_ref[...], t2_ref[...]
    for h in range(MLA_HEADS):
        c0 = h * MLA_QW
        o_ref[:, c0:c0 + LANE] = acc[:, c0:c0 + LANE].astype(BF16)
        x = acc[:, c0 + LANE:c0 + MLA_QW]
        o_ref[:, c0 + LANE:c0 + MLA_QW] = (x * t1 + _half_swap(x) * t2).astype(BF16)


def _uq_proj(proj, w_uq_ext, t1, t2):
    m = proj.shape[0]
    tm = _pick(m, ROW_TILES[:-1])
    n = MLA_HEADS * MLA_QW
    return pl.pallas_call(
        _uq_kernel,
        grid=(m // tm,),
        in_specs=[
            pl.BlockSpec((tm, MLA_RANK), lambda i: (i, EV_CQ_BLOCK)),
            pl.BlockSpec((MLA_RANK, n), lambda i: (0, 0)),
            pl.BlockSpec((tm, LANE), lambda i: (i, 0)),
            pl.BlockSpec((tm, LANE), lambda i: (i, 0)),
        ],
        out_specs=pl.BlockSpec((tm, n), lambda i: (i, 0)),
        out_shape=jax.ShapeDtypeStruct((m, n), BF16),
        compiler_params=_cp("parallel"),
        name="mla_q_up_proj",
    )(proj, w_uq_ext, t1, t2)


def _ukv_kernel(ckv_ref, w_ref, kpe_ref, k_ref, v_ref):
    acc = _dot(ckv_ref[...].astype(BF16), w_ref[...])
    kpe = kpe_ref[...].astype(BF16)
    for h in range(MLA_HEADS):
        c0 = h * MLA_QW
        k_ref[:, c0:c0 + LANE] = acc[:, c0:c0 + LANE].astype(BF16)
        k_ref[:, c0 + LANE:c0 + MLA_QW] = kpe
        v_ref[:, h * MLA_V:(h + 1) * MLA_V] = acc[:, c0 + LANE:c0 + MLA_QW].astype(BF16)


def _ukv_proj(proj, w_ukv, kpe_pad, mp):
    tm = _pick(mp, ROW_TILES[:-1])
    n = MLA_HEADS * MLA_QW
    return pl.pallas_call(
        _ukv_kernel,
        grid=(mp // tm,),
        in_specs=[
            pl.BlockSpec((tm, MLA_RANK), lambda i: (i, EV_CKV_BLOCK)),
            pl.BlockSpec((MLA_RANK, n), lambda i: (0, 0)),
            pl.BlockSpec((tm, LANE), lambda i: (i, 0)),
        ],
        out_specs=[pl.BlockSpec((tm, n), lambda i: (i, 0)),
                   pl.BlockSpec((tm, MLA_HEADS * MLA_V), lambda i: (i, 0))],
        out_shape=[jax.ShapeDtypeStruct((mp, n), BF16),
                   jax.ShapeDtypeStruct((mp, MLA_HEADS * MLA_V), BF16)],
        compiler_params=_cp("parallel"),
        name="mla_kv_up_proj",
    )(proj, w_ukv, kpe_pad)


def _ret_core(q, k, v, g, s0, dec, qd, kd, cd, gn_g, gn_b):
    qb, kb, vb = q.astype(BF16), k.astype(BF16), v.astype(BF16)
    s = _dot_nt(qb, kb) * dec
    o = _dot(s.astype(BF16), vb)
    o = o + _dot(qb, s0.astype(BF16)) * qd
    kk_t = (k * kd).T.astype(BF16)
    s_new = s0 * cd + _dot(kk_t, vb)
    on = _layer_norm(o, gn_g, gn_b)
    return on * (g * jax.nn.sigmoid(g)), s_new


def _ret_prompt_kernel(q_ref, k_ref, v_ref, g_ref, dec_ref, qd_ref, kd_ref, cd_ref, gg_ref, gb_ref,
                       o_ref, s_out_ref, s_ref, *, nc):
    c = pl.program_id(2)

    @pl.when(c == 0)
    def _():
        s_ref[...] = jnp.zeros_like(s_ref)

    for h in range(RET_HEAD_GROUP):
        cols = slice(h * RET_D, (h + 1) * RET_D)
        out, s_new = _ret_core(q_ref[:, cols], k_ref[:, cols], v_ref[:, cols], g_ref[:, cols], s_ref[h],
                               dec_ref[h], qd_ref[h], kd_ref[h], cd_ref[h], gg_ref[h], gb_ref[h])
        o_ref[:, cols] = out
        s_ref[h] = s_new

    @pl.when(c == nc - 1)
    def _():
        s_out_ref[...] = s_ref[...]


RET_HEAD_GROUP = 4


def _ret_prompt(proj, tabs, gn_g, gn_b, batch, seq):
    nc = seq // RET_CHUNK
    hg = RET_HEAD_GROUP
    gw = hg * RET_D
    groups = RET_HEADS // hg

    def tok(section):
        return pl.BlockSpec((RET_CHUNK, gw), lambda b, h, c: (b * nc + c, section * groups + h))

    def per_head(shape):
        return pl.BlockSpec((hg,) + shape, lambda b, h, c: (h, 0, 0))

    dec, qd, kd, cd = tabs
    return pl.pallas_call(
        functools.partial(_ret_prompt_kernel, nc=nc),
        grid=(batch, groups, nc),
        in_specs=[tok(0), tok(1), tok(2), tok(3),
                  per_head((RET_CHUNK, RET_CHUNK)), per_head((RET_CHUNK, LANE)), per_head((RET_CHUNK, LANE)),
                  per_head((1, LANE)), per_head((1, RET_D)), per_head((1, RET_D))],
        out_specs=[pl.BlockSpec((RET_CHUNK, gw), lambda b, h, c: (b * nc + c, h)),
                   pl.BlockSpec((None, hg, RET_D, RET_D), lambda b, h, c: (b, h, 0, 0))],
        out_shape=[jax.ShapeDtypeStruct((batch * seq, RET_HEADS * RET_D), F32),
                   jax.ShapeDtypeStruct((batch, RET_HEADS, RET_D, RET_D), F32)],
        scratch_shapes=[pltpu.VMEM((hg, RET_D, RET_D), F32)],
        compiler_params=_cp("parallel", "parallel", "arbitrary"),
        name="retention_prompt",
    )(proj, proj, proj, proj, dec, qd, kd, cd, gn_g[:, None, :], gn_b[:, None, :])


def _ret_sample_kernel(q_ref, k_ref, v_ref, g_ref, s0_ref, dec_ref, qd_ref, kd_ref, cd_ref, gg_ref, gb_ref,
                       o_ref, s_out_ref, *, ts):
    pad = jnp.zeros((RET_CHUNK - ts, RET_D), F32)

    def padded(ref, h):
        return jnp.concatenate([ref[:, h * RET_D:(h + 1) * RET_D], pad], axis=0)

    for h in range(RET_HEADS):
        out, s_new = _ret_core(padded(q_ref, h), padded(k_ref, h), padded(v_ref, h), padded(g_ref, h),
                               s0_ref[h], dec_ref[h], qd_ref[h], kd_ref[h], cd_ref[h], gg_ref[h], gb_ref[h])
        o_ref[:, h * RET_D:(h + 1) * RET_D] = out[:ts]
        s_out_ref[h] = s_new


def _ret_sample(proj, state, tabs, gn_g, gn_b, mp, bd, ts):
    width = RET_HEADS * RET_D
    row0 = mp // ts

    def tok(j):
        return pl.BlockSpec((ts, width), lambda b: (row0 + b, j))

    def const(a):
        return pl.BlockSpec(a.shape, lambda b: (0,) * a.ndim)

    dec, qd, kd, cd = tabs
    gg, gb = gn_g[:, None, :], gn_b[:, None, :]
    return pl.pallas_call(
        functools.partial(_ret_sample_kernel, ts=ts),
        grid=(bd,),
        in_specs=[tok(0), tok(1), tok(2), tok(3),
                  pl.BlockSpec((None, RET_HEADS, RET_D, RET_D), lambda b: (b, 0, 0, 0)),
                  const(dec), const(qd), const(kd), const(cd), const(gg), const(gb)],
        out_specs=[pl.BlockSpec((ts, width), lambda b: (b, 0)),
                   pl.BlockSpec((None, RET_HEADS, RET_D, RET_D), lambda b: (b, 0, 0, 0))],
        out_shape=[jax.ShapeDtypeStruct((bd * ts, width), F32),
                   jax.ShapeDtypeStruct(state.shape, F32)],
        compiler_params=_cp("parallel"),
        name="retention_sample",
    )(proj, proj, proj, proj, state, dec, qd, kd, cd, gg, gb)


def _ret_tables(chunk_len):
    h = jnp.arange(RET_HEADS, dtype=F32)
    lg = jnp.log(1.0 - jnp.exp2(-5.0 - h))
    idx = jnp.arange(RET_CHUNK, dtype=F32)
    diff = idx[:, None] - idx[None, :]
    dec = jnp.where(diff >= 0, jnp.exp(jnp.maximum(diff, 0.0)[None] * lg[:, None, None]), 0.0)
    qd = jnp.exp((idx + 1.0)[None, :] * lg[:, None])
    kd = jnp.exp((chunk_len - 1.0 - idx)[None, :] * lg[:, None])
    kd = jnp.where(idx[None, :] < chunk_len, kd, 0.0)
    cd = jnp.exp(chunk_len * lg)
    rep = lambda a: jnp.broadcast_to(a[..., None], a.shape + (LANE,))
    return dec, rep(qd), rep(kd), rep(cd[:, None])


def _mla_prompt_kernel(q_ref, k_ref, v_ref, o_ref, *, tq):
    i = pl.program_id(2)
    q = q_ref[...]

    def step(j, carry, masked):
        m, l, acc = carry
        rows = pl.ds(pl.multiple_of(j * tq, tq), tq)
        s = _dot_nt(q, k_ref[rows, :]) * MLA_SCALE
        if masked:
            r = lax.broadcasted_iota(jnp.int32, (tq, tq), 0)
            c = lax.broadcasted_iota(jnp.int32, (tq, tq), 1)
            s = jnp.where(c <= r, s, NEG)
        m_new = jnp.maximum(m, jnp.max(s, axis=1, keepdims=True))
        a = jnp.exp(m - m_new)
        p = jnp.exp(s - m_new)
        l = a * l + jnp.sum(p, axis=1, keepdims=True)
        acc = a * acc + _dot(p.astype(BF16), v_ref[rows, :])
        return m_new, l, acc

    init = (jnp.full((tq, 1), NEG, F32), jnp.zeros((tq, 1), F32), jnp.zeros((tq, MLA_V), F32))
    carry = lax.fori_loop(0, i, lambda j, c: step(j, c, False), init)
    m, l, acc = step(i, carry, True)
    o_ref[...] = acc / l


def _mla_prompt(q_cat, k_cat, v, batch, seq):
    tq = _pick(seq, (512, 256, 128))
    nq = seq // tq
    return pl.pallas_call(
        functools.partial(_mla_prompt_kernel, tq=tq),
        grid=(batch, MLA_HEADS, nq),
        in_specs=[pl.BlockSpec((tq, MLA_QW), lambda b, h, i: (b * nq + i, h)),
                  pl.BlockSpec((seq, MLA_QW), lambda b, h, i: (b, h)),
                  pl.BlockSpec((seq, MLA_V), lambda b, h, i: (b, h))],
        out_specs=pl.BlockSpec((tq, MLA_V), lambda b, h, i: (b * nq + i, h)),
        out_shape=jax.ShapeDtypeStruct((batch * seq, MLA_HEADS * MLA_V), F32),
        compiler_params=_cp("parallel", "parallel", "arbitrary"),
        name="mla_prompt_attention",
    )(q_cat, k_cat, v)


def _qlat_kernel(q_ref, w_ref, o_ref):
    o_ref[...] = _dot(q_ref[...], w_ref[...])


def _q_latent(q_cat, w_uk_t, mp, ms):
    tm = _pick(ms, ROW_TILES[:-1])
    row0 = mp // tm
    return pl.pallas_call(
        _qlat_kernel,
        grid=(MLA_HEADS, ms // tm),
        in_specs=[pl.BlockSpec((tm, MLA_NOPE), lambda h, i: (row0 + i, 2 * h)),
                  pl.BlockSpec((None, MLA_NOPE, MLA_RANK), lambda h, i: (h, 0, 0))],
        out_specs=pl.BlockSpec((None, tm, MLA_RANK), lambda h, i: (h, i, 0)),
        out_shape=jax.ShapeDtypeStruct((MLA_HEADS, ms, MLA_RANK), F32),
        compiler_params=_cp("parallel", "parallel"),
        name="mla_q_latent",
    )(q_cat, w_uk_t)


def _mla_sample_kernel(pt_ref, ql_ref, qp_ref, ckvn_ref, kpen_ref, ckv_hbm, kpe_hbm, o_ref,
                       ckv_buf, kpe_buf, kv_ref, kp_ref, sem, *, layer, npg, nchunk, ts):
    b = pl.program_id(0)
    nb = pl.num_programs(0)
    last = nb * nchunk - 1
    rows = MLA_HEADS * ts

    def chunk_copies(g, slot):
        cps = []
        for j in range(npg):
            page = pt_ref[g * npg + j]
            cps.append(pltpu.make_async_copy(ckv_hbm.at[layer, page], ckv_buf.at[slot, pl.ds(j * PAGE, PAGE)],
                                             sem.at[0, slot]))
            cps.append(pltpu.make_async_copy(kpe_hbm.at[layer, page], kpe_buf.at[slot, j], sem.at[1, slot]))
        return cps

    def ahead(g, k):
        return chunk_copies(jnp.minimum(g + k, last), lax.rem(g + k, MLA_SLOTS))

    @pl.when(b == 0)
    def _():
        for k in range(MLA_SLOTS - 1):
            for cp in ahead(0, k):
                cp.start()

    ql = ql_ref[...].reshape(rows, MLA_RANK).astype(BF16)
    qp = qp_ref[...].astype(BF16)

    def update(state, s, v):
        m, l, acc = state
        m_new = jnp.maximum(m, jnp.max(s, axis=1, keepdims=True))
        a = jnp.exp(m - m_new)
        p = jnp.exp(s - m_new)
        return m_new, a * l + jnp.sum(p, axis=1, keepdims=True), a * acc + _dot(p.astype(BF16), v)

    state = (jnp.full((rows, 1), NEG, F32), jnp.zeros((rows, 1), F32), jnp.zeros((rows, MLA_RANK), F32))
    for c in range(nchunk):
        g = b * nchunk + c
        for cp in ahead(g, MLA_SLOTS - 1):
            cp.start()
        slot = lax.rem(g, MLA_SLOTS)
        for cp in chunk_copies(g, slot):
            cp.wait()
        kvb, kpb = kv_ref.at[c % 2], kp_ref.at[c % 2]
        kvb[...] = ckv_buf[slot].astype(BF16)
        for j in range(npg):
            kpb[:, j * PAGE:(j + 1) * PAGE] = kpe_buf[slot, j].astype(BF16)
        kv = kvb[...]
        state = update(state, (_dot_nt(ql, kv) + _dot(qp, kpb[...])) * MLA_SCALE, kv)

    ckv = jnp.concatenate([ckvn_ref[...], jnp.zeros((PAGE - ts, MLA_RANK), F32)], axis=0).astype(BF16)
    kpe = jnp.concatenate([kpen_ref[...], jnp.zeros((PAGE - ts, MLA_ROPE), F32)], axis=0).astype(BF16)
    s = (_dot_nt(ql, ckv) + _dot_nt(qp, kpe)) * MLA_SCALE
    t = lax.broadcasted_iota(jnp.int32, (rows, PAGE), 0) % ts
    key = lax.broadcasted_iota(jnp.int32, (rows, PAGE), 1)
    _, l, acc = update(state, jnp.where(key <= t, s, NEG), ckv)
    o_ref[...] = acc / l

    @pl.when(b == nb - 1)
    def _():
        for k in range(1, MLA_SLOTS):
            for cp in ahead(last, k):
                cp.wait()


MLA_CHUNK_PAGES = (16, 8, 4, 2, 1)
MLA_SLOTS = 3


def _mla_sample(page_table_flat, q_lat, q_pe, ckv_new, kpe_new, cache_ckv, cache_kpe_t, layer, bd, ts, n_pages):
    npg = _pick(n_pages, MLA_CHUNK_PAGES)
    nchunk = n_pages // npg
    rows = MLA_HEADS * ts
    in_specs = [pl.BlockSpec((MLA_HEADS, ts, MLA_RANK), lambda b, pt: (0, b, 0)),
                pl.BlockSpec((None, rows, MLA_ROPE), lambda b, pt: (b, 0, 0)),
                pl.BlockSpec((None, ts, MLA_RANK), lambda b, pt: (b, 0, 0)),
                pl.BlockSpec((None, ts, MLA_ROPE), lambda b, pt: (b, 0, 0)),
                pl.BlockSpec(memory_space=pl.ANY),
                pl.BlockSpec(memory_space=pl.ANY)]
    return pl.pallas_call(
        functools.partial(_mla_sample_kernel, layer=layer, npg=npg, nchunk=nchunk, ts=ts),
        grid_spec=pltpu.PrefetchScalarGridSpec(
            num_scalar_prefetch=1,
            grid=(bd,),
            in_specs=in_specs,
            out_specs=pl.BlockSpec((None, rows, MLA_RANK), lambda b, pt: (b, 0, 0)),
            scratch_shapes=[pltpu.VMEM((MLA_SLOTS, npg * PAGE, MLA_RANK), F32),
                            pltpu.VMEM((MLA_SLOTS, npg, MLA_ROPE, PAGE), F32),
                            pltpu.VMEM((2, npg * PAGE, MLA_RANK), BF16), pltpu.VMEM((2, MLA_ROPE, npg * PAGE), BF16),
                            pltpu.SemaphoreType.DMA((2, MLA_SLOTS))],
        ),
        out_shape=jax.ShapeDtypeStruct((bd, rows, MLA_RANK), F32),
        compiler_params=_cp("arbitrary"),
        name="mla_sample_attention",
    )(page_table_flat, q_lat, q_pe, ckv_new, kpe_new, cache_ckv, cache_kpe_t)


def _uv_kernel(o_ref, w_ref, out_ref, *, ts):
    tb = o_ref.shape[0]
    o = o_ref[...].reshape(tb * ts, MLA_RANK).astype(BF16)
    out_ref[...] = _dot(o, w_ref[...])


def _mla_value_up(o_lat, w_uv, bd, ts):
    tb = _pick(bd, (64, 32, 16, 8, 4, 2, 1))
    return pl.pallas_call(
        functools.partial(_uv_kernel, ts=ts),
        grid=(MLA_HEADS, bd // tb),
        in_specs=[pl.BlockSpec((tb, None, ts, MLA_RANK), lambda h, i: (i, h, 0, 0)),
                  pl.BlockSpec((None, MLA_RANK, MLA_V), lambda h, i: (h, 0, 0))],
        out_specs=pl.BlockSpec((tb * ts, MLA_V), lambda h, i: (i, h)),
        out_shape=jax.ShapeDtypeStruct((bd * ts, MLA_HEADS * MLA_V), F32),
        compiler_params=_cp("parallel", "parallel"),
        name="mla_value_up_proj",
    )(o_lat, w_uv)


def _mm_kernel(x_ref, w_ref, o_ref, xb_ref):
    @pl.when(pl.program_id(1) == 0)
    def _():
        xb_ref[...] = x_ref[...].astype(BF16)

    o_ref[...] = _dot(xb_ref[...], w_ref[...])


def _matmul(x, w):
    m, d = x.shape
    n = w.shape[1]
    tm = _pick(m, PROJ_ROW_TILES)
    tn = _pick(n, (512, 256, 128))
    return pl.pallas_call(
        _mm_kernel,
        grid=(m // tm, n // tn),
        in_specs=[pl.BlockSpec((tm, d), lambda i, j: (i, 0)),
                  pl.BlockSpec((d, tn), lambda i, j: (0, j))],
        out_specs=pl.BlockSpec((tm, tn), lambda i, j: (i, j)),
        out_shape=jax.ShapeDtypeStruct((m, n), F32),
        scratch_shapes=[pltpu.VMEM((tm, d), BF16)],
        compiler_params=_cp("parallel", "arbitrary"),
        name="sb_in_proj",
    )(x, w)


SB_Q_COLS = SB_HEADS * SB_DIM
SB_KV_COLS = SB_KV_HEADS * SB_DIM
SB_TK = 128
SB_EXIT = -104.0


def _sb_block(z, mask, carry, tri_ones):
    r = z.shape[0]
    sp = jnp.maximum(z, 0.0) + jnp.log1p(jnp.exp(-jnp.abs(z)))
    if mask is not None:
        sp_m = jnp.where(mask, sp, 0.0)
    else:
        sp_m = sp
    hi = sp_m.astype(BF16)
    lo = (sp_m - hi.astype(F32)).astype(BF16)
    cs = _dot(jnp.concatenate([hi, lo], axis=0), tri_ones)
    cs = cs[:r] + cs[r:]
    a = jnp.exp(z - sp + carry - cs[:, :SB_TK])
    if mask is not None:
        a = jnp.where(mask, a, 0.0)
    return a, carry - cs[:, SB_TK:]


def _sb_live(carry):
    return (jnp.max(carry) > SB_EXIT).astype(jnp.int32)


def _sb_prompt_kernel(q_ref, k_ref, v_ref, to_ref, o_ref, *, tq):
    i = pl.program_id(2)
    rows = SB_GROUP * tq
    q = jnp.concatenate([q_ref[:, g * SB_DIM:(g + 1) * SB_DIM] for g in range(SB_GROUP)], axis=0).astype(BF16)
    tri_ones = to_ref[...]

    def step(j, carry, acc, mask):
        ks = pl.ds(pl.multiple_of(j * SB_TK, SB_TK), SB_TK)
        z = _dot_nt(q, k_ref[ks, :].astype(BF16)) * SB_SCALE
        a, carry = _sb_block(z, mask, carry, tri_ones)
        return carry, acc + _dot(a.astype(BF16), v_ref[ks, :].astype(BF16))

    t = lax.broadcasted_iota(jnp.int32, (rows, SB_TK), 0) % tq
    key = lax.broadcasted_iota(jnp.int32, (rows, SB_TK), 1)
    carry, acc = step(i, jnp.zeros((rows, SB_TK), F32), jnp.zeros((rows, SB_DIM), F32), key < t)

    def body(state):
        j, _, carry, acc = state
        carry, acc = step(j, carry, acc, None)
        return j - 1, _sb_live(carry), carry, acc

    _, _, carry, acc = lax.while_loop(lambda s: (s[0] >= 0) & (s[1] > 0), body,
                                      (i - 1, _sb_live(carry), carry, acc))
    for g in range(SB_GROUP):
        o_ref[:, g * SB_DIM:(g + 1) * SB_DIM] = acc[g * tq:(g + 1) * tq]


def _sb_prompt(proj, tri_ones, batch, seq):
    tq = SB_TK
    nq = seq // tq
    gw = SB_GROUP * SB_DIM
    kb0 = SB_Q_COLS // SB_DIM
    vb0 = kb0 + SB_KV_HEADS
    return pl.pallas_call(
        functools.partial(_sb_prompt_kernel, tq=tq),
        grid=(batch, SB_KV_HEADS, nq),
        in_specs=[pl.BlockSpec((tq, gw), lambda b, h, i: (b * nq + i, h)),
                  pl.BlockSpec((seq, SB_DIM), lambda b, h, i: (b, kb0 + h)),
                  pl.BlockSpec((seq, SB_DIM), lambda b, h, i: (b, vb0 + h)),
                  pl.BlockSpec((SB_TK, 2 * SB_TK), lambda b, h, i: (0, 0))],
        out_specs=pl.BlockSpec((tq, gw), lambda b, h, i: (b * nq + i, h)),
        out_shape=jax.ShapeDtypeStruct((batch * seq, SB_Q_COLS), F32),
        compiler_params=_cp("parallel", "parallel", "arbitrary"),
        name="sb_prompt_attention",
    )(proj, proj, proj, tri_ones)


def _sb_sample_kernel(pt_ref, q_ref, kn_ref, vn_ref, to_ref, kc_ref, vc_ref, o_ref,
                      kfirst, vfirst, kmore, vmore, sem_first, sem_more, *, ts, npg, nchunk):
    b = pl.program_id(0)
    nb = pl.num_programs(0)
    n_pages = npg * nchunk
    page_rows = PAGE * SB_KV_HEADS
    hrows = SB_GROUP * ts
    rows = SB_KV_HEADS * hrows
    q = jnp.concatenate([q_ref[:, h * SB_DIM:(h + 1) * SB_DIM] for h in range(SB_HEADS)], axis=0).astype(BF16)
    tri_ones = to_ref[...]

    def chunk_copies(seq, c, kbuf, vbuf, ksem, vsem):
        cps = []
        for j in range(npg):
            page = pt_ref[seq * n_pages + n_pages - 1 - (c * npg + j)]
            rows_j = pl.ds(j * page_rows, page_rows)
            cps.append(pltpu.make_async_copy(kc_ref.at[page], kbuf.at[rows_j], ksem))
            cps.append(pltpu.make_async_copy(vc_ref.at[page], vbuf.at[rows_j], vsem))
        return cps

    def first_copies(seq, slot):
        return chunk_copies(seq, 0, kfirst.at[slot], vfirst.at[slot], sem_first.at[0, slot], sem_first.at[1, slot])

    slot = lax.rem(b, 2)

    @pl.when(b == 0)
    def _():
        for cp in first_copies(b, slot):
            cp.start()

    nxt = first_copies(jnp.minimum(b + 1, nb - 1), 1 - slot)
    for cp in nxt:
        cp.start()

    def block(k_of, v_of, mask, carry, acc):
        z = jnp.concatenate(
            [_dot_nt(q[h * hrows:(h + 1) * hrows], k_of(h).astype(BF16)) for h in range(SB_KV_HEADS)],
            axis=0) * SB_SCALE
        a, carry = _sb_block(z, mask, carry, tri_ones)
        ab = a.astype(BF16)
        upd = jnp.concatenate(
            [_dot(ab[h * hrows:(h + 1) * hrows], v_of(h).astype(BF16)) for h in range(SB_KV_HEADS)],
            axis=0)
        return carry, acc + upd

    zpad = jnp.zeros((PAGE - ts, SB_KV_COLS), F32)
    kn = jnp.concatenate([kn_ref[...], zpad], axis=0)
    vn = jnp.concatenate([vn_ref[...], zpad], axis=0)
    t = lax.broadcasted_iota(jnp.int32, (rows, SB_TK), 0) % ts
    key = lax.broadcasted_iota(jnp.int32, (rows, SB_TK), 1)
    carry, acc = block(lambda h: kn[:, h * SB_DIM:(h + 1) * SB_DIM], lambda h: vn[:, h * SB_DIM:(h + 1) * SB_DIM],
                       key < t, jnp.zeros((rows, SB_TK), F32), jnp.zeros((rows, SB_DIM), F32))

    def chunk_blocks(k_rows, v_rows, carry, acc):
        for j in range(npg):
            def head(read, h, j=j):
                return read(pl.ds(j * page_rows + h, PAGE, stride=SB_KV_HEADS))
            carry, acc = block(lambda h: head(k_rows, h), lambda h: head(v_rows, h), None, carry, acc)
        return carry, acc

    for cp in first_copies(b, slot):
        cp.wait()
    carry, acc = chunk_blocks(lambda r: kfirst[slot, r, :], lambda r: vfirst[slot, r, :], carry, acc)

    def body(state):
        c, _, carry, acc = state
        cps = chunk_copies(b, c, kmore, vmore, sem_more.at[0], sem_more.at[1])
        for cp in cps:
            cp.start()
        for cp in cps:
            cp.wait()
        carry, acc = chunk_blocks(lambda r: kmore[r, :], lambda r: vmore[r, :], carry, acc)
        return c + 1, _sb_live(carry), carry, acc

    _, _, carry, acc = lax.while_loop(lambda s: (s[0] < nchunk) & (s[1] > 0), body,
                                      (jnp.int32(1), _sb_live(carry), carry, acc))

    for h in range(SB_HEADS):
        o_ref[:, h * SB_DIM:(h + 1) * SB_DIM] = acc[h * ts:(h + 1) * ts]

    @pl.when(b == nb - 1)
    def _():
        for cp in nxt:
            cp.wait()


SB_CHUNK_PAGES = (2, 1)


def _sb_sample(page_table_flat, proj, tri_ones, cache_k, cache_v, mp, bd, ts, n_pages):
    row0 = mp // ts
    kb0 = SB_Q_COLS // SB_KV_COLS
    npg = _pick(n_pages, SB_CHUNK_PAGES)
    chunk_rows = npg * PAGE * SB_KV_HEADS
    in_specs = [pl.BlockSpec((ts, SB_Q_COLS), lambda b, pt: (row0 + b, 0)),
                pl.BlockSpec((ts, SB_KV_COLS), lambda b, pt: (row0 + b, kb0)),
                pl.BlockSpec((ts, SB_KV_COLS), lambda b, pt: (row0 + b, kb0 + 1)),
                pl.BlockSpec((SB_TK, 2 * SB_TK), lambda b, pt: (0, 0)),
                pl.BlockSpec(memory_space=pl.ANY),
                pl.BlockSpec(memory_space=pl.ANY)]
    return pl.pallas_call(
        functools.partial(_sb_sample_kernel, ts=ts, npg=npg, nchunk=n_pages // npg),
        grid_spec=pltpu.PrefetchScalarGridSpec(
            num_scalar_prefetch=1,
            grid=(bd,),
            in_specs=in_specs,
            out_specs=pl.BlockSpec((ts, SB_Q_COLS), lambda b, pt: (b, 0)),
            scratch_shapes=[pltpu.VMEM((2, chunk_rows, SB_DIM), F32), pltpu.VMEM((2, chunk_rows, SB_DIM), F32),
                            pltpu.VMEM((chunk_rows, SB_DIM), F32), pltpu.VMEM((chunk_rows, SB_DIM), F32),
                            pltpu.SemaphoreType.DMA((2, 2)), pltpu.SemaphoreType.DMA((2,))],
        ),
        out_shape=jax.ShapeDtypeStruct((bd * ts, SB_Q_COLS), F32),
        compiler_params=_cp("arbitrary"),
        name="sb_sample_attention",
    )(page_table_flat, proj, proj, proj, tri_ones, cache_k, cache_v)


def _rope_tables(pos):
    p = pos.astype(F32)[:, None]
    inv_r = 1.0 / (ROPE_BASE ** jnp.linspace(0.0, 1.0, RET_D // 2, dtype=F32))
    ang = p * inv_r[None, :]
    c, s = jnp.cos(ang), jnp.sin(ang)
    cr = jnp.concatenate([c, c], axis=1)
    sr = jnp.concatenate([-s, s], axis=1)
    inv_m = 1.0 / (ROPE_BASE ** (jnp.arange(0, MLA_ROPE, 2, dtype=F32) / MLA_ROPE))
    ang = p * inv_m[None, :]
    c, s = jnp.cos(ang), jnp.sin(ang)
    z = jnp.zeros((pos.shape[0], LANE - MLA_ROPE), F32)
    t1 = jnp.concatenate([c, c, z], axis=1)
    t2 = jnp.concatenate([-s, s, z], axis=1)
    return cr, sr, t1, t2


def _swap_halves(w):
    half = w.shape[-1] // 2
    return jnp.concatenate([w[..., half:], w[..., :half]], axis=-1)


def kernel(x_prompt, x_sample, state_ret, cache_mla_ckv, cache_mla_kpe, cache_sb_k, cache_sb_v, page_table,
           ln_g, ln_b, ffn_w_gate, ffn_w_up, ffn_w_down,
           ev_w_in, ev_q_norm, ev_w_uq, ev_kv_norm, ev_w_ukv, ev_gn_g, ev_gn_b, ev_w_o,
           od_w_in, od_w_o):
    batch, seq, d = x_prompt.shape
    bd, ts, _ = x_sample.shape
    n_pages = page_table.shape[1]
    past = n_pages * PAGE
    mp, ms = batch * seq, bd * ts
    assert seq % RET_CHUNK == 0 and ts <= 8 and mp % ts == 0

    x = jnp.concatenate([x_prompt.reshape(mp, d), x_sample.reshape(ms, d)], axis=0)
    pos = jnp.concatenate([jnp.tile(jnp.arange(seq), batch), jnp.tile(past + jnp.arange(ts), bd)])
    cr, sr, t1, t2 = _rope_tables(pos)
    pt_flat = page_table.reshape(-1)
    tri = (jnp.arange(SB_TK)[:, None] > jnp.arange(SB_TK)[None, :]).astype(BF16)
    tri_ones = jnp.concatenate([tri, jnp.ones((SB_TK, SB_TK), BF16)], axis=1)
    tabs_p = _ret_tables(float(RET_CHUNK))
    tabs_s = _ret_tables(float(ts))

    wg, wu, wd = ffn_w_gate.astype(BF16), ffn_w_up.astype(BF16), ffn_w_down.astype(BF16)

    outs = {k: [] for k in ("ret_p", "ret_s", "ckv_p", "ckv_s", "kpe_p", "kpe_s", "sbk_p", "sbk_s", "sbv_p", "sbv_s")}
    for l in range(DEPTH):
        x = _ffn_half_step(x, wg, wu, wd, l, 0, ln_g[l, 0], ln_b[l, 0])
        if l % 2 == 0:
            e = l // 2
            w_in = ev_w_in[e]
            w_main = w_in[:, :EV_MAIN_COLS].astype(BF16)
            w_kpe = w_in[:, EV_MAIN_COLS:]
            w_kpe = jnp.concatenate([w_kpe, _swap_halves(w_kpe)], axis=1).astype(BF16)
            proj = _ev_proj(x, w_main, cr, sr, ev_q_norm[e], ev_kv_norm[e])
            kpe_pad = _kpe_proj(x, w_kpe, t1, t2)

            w_uq = ev_w_uq[e].reshape(MLA_RANK, MLA_HEADS, MLA_NOPE + MLA_ROPE)
            w_uq_ext = jnp.concatenate([w_uq, _swap_halves(w_uq[..., MLA_NOPE:])], axis=-1)
            q_cat = _uq_proj(proj, w_uq_ext.reshape(MLA_RANK, MLA_HEADS * MLA_QW).astype(BF16), t1, t2)

            w_ukv = ev_w_ukv[e]
            k_cat, v_p = _ukv_proj(proj, w_ukv.astype(BF16), kpe_pad, mp)

            ret_out_p, s_p = _ret_prompt(proj, tabs_p, ev_gn_g[e], ev_gn_b[e], batch, seq)
            ret_out_s, s_s = _ret_sample(proj, state_ret[e], tabs_s, ev_gn_g[e], ev_gn_b[e], mp, bd, ts)
            mla_out_p = _mla_prompt(q_cat, k_cat, v_p, batch, seq)

            w3 = w_ukv.reshape(MLA_RANK, MLA_HEADS, MLA_NOPE + MLA_V)
            w_uk_t = w3[..., :MLA_NOPE].transpose(1, 2, 0).astype(BF16)
            w_uv = w3[..., MLA_NOPE:].transpose(1, 0, 2).astype(BF16)
            q_lat = _q_latent(q_cat, w_uk_t, mp, ms)
            q_pe_s = q_cat[mp:].reshape(bd, ts, MLA_HEADS, MLA_QW)[..., MLA_NOPE:MLA_NOPE + MLA_ROPE]
            q_pe_s = q_pe_s.transpose(0, 2, 1, 3).reshape(bd, MLA_HEADS * ts, MLA_ROPE)
            ckv = proj[:, EV_CKV_BLOCK * EV_BLOCK:(EV_CKV_BLOCK + 1) * EV_BLOCK]
            kpe = kpe_pad[:, :MLA_ROPE]
            ckv_s = ckv[mp:].reshape(bd, ts, MLA_RANK)
            kpe_s = kpe[mp:].reshape(bd, ts, MLA_ROPE)
            o_lat = _mla_sample(pt_flat, q_lat, q_pe_s, ckv_s, kpe_s, cache_mla_ckv,
                                jnp.swapaxes(cache_mla_kpe, 2, 3), e, bd, ts, n_pages)
            mla_out_s = _mla_value_up(o_lat.reshape(bd, MLA_HEADS, ts, MLA_RANK), w_uv, bd, ts)

            w_o = ev_w_o[e].astype(BF16)
            n_ret = RET_HEADS * RET_D
            x = _out_proj_ln([(ret_out_p, ret_out_s), (mla_out_p, mla_out_s)],
                             [w_o[:n_ret], w_o[n_ret:]], x, ln_g[l, 1], ln_b[l, 1])
            outs["ret_p"].append(s_p)
            outs["ret_s"].append(s_s)
            outs["ckv_p"].append(ckv[:mp].reshape(batch, seq, MLA_RANK))
            outs["ckv_s"].append(ckv_s)
            outs["kpe_p"].append(kpe[:mp].reshape(batch, seq, MLA_ROPE))
            outs["kpe_s"].append(kpe_s)
        else:
            o = l // 2
            proj = _matmul(x, od_w_in[o].astype(BF16))
            n_kv = cache_sb_k.shape[1]
            cache_k = cache_sb_k[o].reshape(n_kv, PAGE * SB_KV_HEADS, SB_DIM)
            cache_v = cache_sb_v[o].reshape(n_kv, PAGE * SB_KV_HEADS, SB_DIM)
            att_p = _sb_prompt(proj, tri_ones, batch, seq)
            att_s = _sb_sample(pt_flat, proj, tri_ones, cache_k, cache_v, mp, bd, ts, n_pages)
            x = _out_proj_ln([(att_p, att_s)], [od_w_o[o].astype(BF16)], x, ln_g[l, 1], ln_b[l, 1])
            k = proj[:, SB_Q_COLS:SB_Q_COLS + SB_KV_COLS]
            v = proj[:, SB_Q_COLS + SB_KV_COLS:]
            outs["sbk_p"].append(k[:mp].reshape(batch, seq, SB_KV_HEADS, SB_DIM))
            outs["sbk_s"].append(k[mp:].reshape(bd, ts, SB_KV_HEADS, SB_DIM))
            outs["sbv_p"].append(v[:mp].reshape(batch, seq, SB_KV_HEADS, SB_DIM))
            outs["sbv_s"].append(v[mp:].reshape(bd, ts, SB_KV_HEADS, SB_DIM))
        x = _ffn_half_step(x, wg, wu, wd, l, 1, ln_g[l, 2], ln_b[l, 2])

    y_p = x[:mp].reshape(batch, seq, d)
    y_s = x[mp:].reshape(bd, ts, d)
    return (y_p, y_s, jnp.stack(outs["ret_p"]), jnp.stack(outs["ret_s"]),
            jnp.stack(outs["ckv_p"]), jnp.stack(outs["ckv_s"]), jnp.stack(outs["kpe_p"]), jnp.stack(outs["kpe_s"]),
            jnp.stack(outs["sbk_p"]), jnp.stack(outs["sbk_s"]), jnp.stack(outs["sbv_p"]), jnp.stack(outs["sbv_s"]))
```

```python
import functools

import jax
import jax.numpy as jnp
import numpy as np
from jax import lax
from jax.experimental import pallas as pl
from jax.experimental.pallas import tpu as pltpu

F32 = jnp.float32
BF16 = jnp.bfloat16

DEPTH = 2
PAGE = 128
RET_HEADS = 8
RET_D = 128
MLA_HEADS = 8
MLA_NOPE = 128
MLA_ROPE = 64
MLA_V = 128
MLA_RANK = 512
MLA_SCALE = (MLA_NOPE + MLA_ROPE) ** -0.5
SB_HEADS = 16
SB_KV_HEADS = 4
SB_GROUP = SB_HEADS // SB_KV_HEADS
SB_DIM = 128
SB_SCALE = SB_DIM ** -0.5
ROPE_BASE = 10000.0
EPS = 1e-5
ALPHA = (2 * DEPTH) ** 0.25
RET_CHUNK = 128
NEG = -1e30

LANE = 128
VMEM_LIMIT = 56 * 1024 * 1024
ROW_TILES = (512, 256, 128, 64, 32, 16, 8)
FFN_ROW_TILES = (768,) + ROW_TILES
PROJ_ROW_TILES = (1024,) + ROW_TILES


def _pick(n, cands):
    for c in cands:
        if n % c == 0:
            return c
    raise ValueError(f"no tile for {n}")


def _cp(*sem):
    return pltpu.CompilerParams(dimension_semantics=sem, vmem_limit_bytes=VMEM_LIMIT)


def _dot(a, b):
    return jnp.dot(a, b, preferred_element_type=F32)


def _dot_nt(a, b):
    return lax.dot_general(a, b, (((1,), (1,)), ((), ())), preferred_element_type=F32)


def _layer_norm(y, g, b):
    mu = jnp.mean(y, axis=-1, keepdims=True)
    d = y - mu
    var = jnp.mean(d * d, axis=-1, keepdims=True)
    return d * lax.rsqrt(var + EPS) * g + b


def _rms_norm(y, g):
    return y * lax.rsqrt(jnp.mean(y * y, axis=-1, keepdims=True) + EPS) * g


def _half_swap(x):
    return pltpu.roll(x, LANE // 2, 1)


def _ffn_kernel(x_ref, wg_ref, wu_ref, wd_ref, g_ref, b_ref, o_ref, xb_ref, acc_ref, *, nf):
    f = pl.program_id(1)

    @pl.when(f == 0)
    def _():
        xb_ref[...] = x_ref[...].astype(BF16)
        acc_ref[...] = jnp.zeros_like(acc_ref)

    xb = xb_ref[...]
    g = _dot(xb, wg_ref[...])
    u = _dot(xb, wu_ref[...])
    a = (g * jax.nn.sigmoid(g) * u).astype(BF16)
    acc_ref[...] += _dot(a, wd_ref[...])

    @pl.when(f == nf - 1)
    def _():
        y = ALPHA * x_ref[...] + 0.5 * acc_ref[...]
        o_ref[...] = _layer_norm(y, g_ref[...], b_ref[...])


def _ffn_half_step(x, wg, wu, wd, layer, half, g, b):
    m, d = x.shape
    ff = wg.shape[-1]
    tm = _pick(m, FFN_ROW_TILES)
    tf = _pick(ff, (512, 256, 128))
    nf = ff // tf
    return pl.pallas_call(
        functools.partial(_ffn_kernel, nf=nf),
        grid=(m // tm, nf),
        in_specs=[
            pl.BlockSpec((tm, d), lambda i, f: (i, 0)),
            pl.BlockSpec((None, None, d, tf), lambda i, f: (layer, half, 0, f)),
            pl.BlockSpec((None, None, d, tf), lambda i, f: (layer, half, 0, f)),
            pl.BlockSpec((None, None, tf, d), lambda i, f: (layer, half, f, 0)),
            pl.BlockSpec((1, d), lambda i, f: (0, 0)),
            pl.BlockSpec((1, d), lambda i, f: (0, 0)),
        ],
        out_specs=pl.BlockSpec((tm, d), lambda i, f: (i, 0)),
        out_shape=jax.ShapeDtypeStruct((m, d), F32),
        scratch_shapes=[pltpu.VMEM((tm, d), BF16), pltpu.VMEM((tm, d), F32)],
        compiler_params=_cp("parallel", "arbitrary"),
        name="ffn_half_step",
    )(x, wg, wu, wd, g.reshape(1, d), b.reshape(1, d))


def _out_ln_kernel(*refs, n_parts, prompt_blocks):
    p_refs = refs[:n_parts]
    s_refs = refs[n_parts:2 * n_parts]
    w_refs = refs[2 * n_parts:3 * n_parts]
    x_ref, g_ref, b_ref, o_ref = refs[3 * n_parts:]
    is_prompt = pl.program_id(0) < prompt_blocks
    acc = None
    for p_ref, s_ref, w_ref in zip(p_refs, s_refs, w_refs):
        a = jnp.where(is_prompt, p_ref[...], s_ref[...]).astype(BF16)
        part = _dot(a, w_ref[...])
        acc = part if acc is None else acc + part
    y = ALPHA * x_ref[...] + acc
    o_ref[...] = _layer_norm(y, g_ref[...], b_ref[...])


def _out_proj_ln(parts, weights, x, g, b):
    m, d = x.shape
    mp, ms = parts[0][0].shape[0], parts[0][1].shape[0]
    tm = _pick(int(np.gcd(mp, ms)), ROW_TILES)
    npb = mp // tm
    n = len(parts)
    in_specs = [pl.BlockSpec((tm, p.shape[1]), lambda i: (jnp.minimum(i, npb - 1), 0)) for p, _ in parts]
    in_specs += [pl.BlockSpec((tm, s.shape[1]), lambda i: (jnp.maximum(i - npb, 0), 0)) for _, s in parts]
    in_specs += [pl.BlockSpec(w.shape, lambda i: (0, 0)) for w in weights]
    in_specs += [pl.BlockSpec((tm, d), lambda i: (i, 0)),
                 pl.BlockSpec((1, d), lambda i: (0, 0)),
                 pl.BlockSpec((1, d), lambda i: (0, 0))]
    return pl.pallas_call(
        functools.partial(_out_ln_kernel, n_parts=n, prompt_blocks=npb),
        grid=(m // tm,),
        in_specs=in_specs,
        out_specs=pl.BlockSpec((tm, d), lambda i: (i, 0)),
        out_shape=jax.ShapeDtypeStruct((m, d), F32),
        compiler_params=_cp("parallel"),
        name="out_proj_ln",
    )(*[p for p, _ in parts], *[s for _, s in parts], *weights, x, g.reshape(1, d), b.reshape(1, d))


EV_BLOCK = 512
EV_Q_BLOCKS = (0, 2)
EV_K_BLOCKS = (2, 4)
EV_CQ_BLOCK = 8
EV_CKV_BLOCK = 9
EV_MAIN_COLS = 10 * EV_BLOCK


def _ev_proj_kernel(x_ref, w_ref, cr_ref, sr_ref, gq_ref, gkv_ref, o_ref, xb_ref):
    j = pl.program_id(1)

    @pl.when(j == 0)
    def _():
        xb_ref[...] = x_ref[...].astype(BF16)

    acc = _dot(xb_ref[...], w_ref[...])

    def rope(scale):
        c, s = cr_ref[...], sr_ref[...]
        for h in range(EV_BLOCK // LANE):
            blk = acc[:, h * LANE:(h + 1) * LANE]
            r = blk * c + _half_swap(blk) * s
            o_ref[:, h * LANE:(h + 1) * LANE] = r if scale is None else r * scale

    @pl.when(j < EV_Q_BLOCKS[1])
    def _():
        rope(None)

    @pl.when((j >= EV_K_BLOCKS[0]) & (j < EV_K_BLOCKS[1]))
    def _():
        rope(RET_D ** -0.5)

    @pl.when((j >= EV_K_BLOCKS[1]) & (j < EV_CQ_BLOCK))
    def _():
        o_ref[...] = acc

    @pl.when(j == EV_CQ_BLOCK)
    def _():
        o_ref[...] = _rms_norm(acc, gq_ref[...])

    @pl.when(j == EV_CKV_BLOCK)
    def _():
        o_ref[...] = _rms_norm(acc, gkv_ref[...])


def _ev_proj(x, w_main, cr, sr, gq, gkv):
    m, d = x.shape
    tm = _pick(m, PROJ_ROW_TILES)
    nb = EV_MAIN_COLS // EV_BLOCK
    return pl.pallas_call(
        _ev_proj_kernel,
        grid=(m // tm, nb),
        in_specs=[
            pl.BlockSpec((tm, d), lambda i, j: (i, 0)),
            pl.BlockSpec((d, EV_BLOCK), lambda i, j: (0, j)),
            pl.BlockSpec((tm, LANE), lambda i, j: (i, 0)),
            pl.BlockSpec((tm, LANE), lambda i, j: (i, 0)),
            pl.BlockSpec((1, MLA_RANK), lambda i, j: (0, 0)),
            pl.BlockSpec((1, MLA_RANK), lambda i, j: (0, 0)),
        ],
        out_specs=pl.BlockSpec((tm, EV_BLOCK), lambda i, j: (i, j)),
        out_shape=jax.ShapeDtypeStruct((m, EV_MAIN_COLS), F32),
        scratch_shapes=[pltpu.VMEM((tm, d), BF16)],
        compiler_params=_cp("parallel", "arbitrary"),
        name="even_in_proj",
    )(x, w_main, cr, sr, gq.reshape(1, -1), gkv.reshape(1, -1))


def _kpe_kernel(x_ref, w_ref, t1_ref, t2_ref, o_ref):
    acc = _dot(x_ref[...].astype(BF16), w_ref[...])
    o_ref[...] = acc * t1_ref[...] + _half_swap(acc) * t2_ref[...]


def _kpe_proj(x, w_kpe, t1, t2):
    m, d = x.shape
    tm = _pick(m, ROW_TILES)
    return pl.pallas_call(
        _kpe_kernel,
        grid=(m // tm,),
        in_specs=[
            pl.BlockSpec((tm, d), lambda i: (i, 0)),
            pl.BlockSpec((d, LANE), lambda i: (0, 0)),
            pl.BlockSpec((tm, LANE), lambda i: (i, 0)),
            pl.BlockSpec((tm, LANE), lambda i: (i, 0)),
        ],
        out_specs=pl.BlockSpec((tm, LANE), lambda i: (i, 0)),
        out_shape=jax.ShapeDtypeStruct((m, LANE), F32),
        compiler_params=_cp("parallel"),
        name="kpe_proj",
    )(x, w_kpe, t1, t2)


MLA_QW = 2 * LANE


def _uq_kernel(cq_ref, w_ref, t1_ref, t2_ref, o_ref):
    acc = _dot(cq_ref[...].astype(BF16), w_ref[...])
    t1, t2 = t1_ref[...], t2_ref[...]
    for h in range(MLA_HEADS):
        c0 = h * MLA_QW
        o_ref[:, c0:c0 + LANE] = acc[:, c0:c0 + LANE].astype(BF16)
        x = acc[:, c0 + LANE:c0 + MLA_QW]
        o_ref[:, c0 + LANE:c0 + MLA_QW] = (x * t1 + _half_swap(x) * t2).astype(BF16)


def _uq_proj(proj, w_uq_ext, t1, t2):
    m = proj.shape[0]
    tm = _pick(m, ROW_TILES[:-1])
    n = MLA_HEADS * MLA_QW
    return pl.pallas_call(
        _uq_kernel,
        grid=(m // tm,),
        in_specs=[
            pl.BlockSpec((tm, MLA_RANK), lambda i: (i, EV_CQ_BLOCK)),
            pl.BlockSpec((MLA_RANK, n), lambda i: (0, 0)),
            pl.BlockSpec((tm, LANE), lambda i: (i, 0)),
            pl.BlockSpec((tm, LANE), lambda i: (i, 0)),
        ],
        out_specs=pl.BlockSpec((tm, n), lambda i: (i, 0)),
        out_shape=jax.ShapeDtypeStruct((m, n), BF16),
        compiler_params=_cp("parallel"),
        name="mla_q_up_proj",
    )(proj, w_uq_ext, t1, t2)


def _ukv_kernel(ckv_ref, w_ref, kpe_ref, k_ref, v_ref):
    acc = _dot(ckv_ref[...].astype(BF16), w_ref[...])
    kpe = kpe_ref[...].astype(BF16)
    for h in range(MLA_HEADS):
        c0 = h * MLA_QW
        k_ref[:, c0:c0 + LANE] = acc[:, c0:c0 + LANE].astype(BF16)
        k_ref[:, c0 + LANE:c0 + MLA_QW] = kpe
        v_ref[:, h * MLA_V:(h + 1) * MLA_V] = acc[:, c0 + LANE:c0 + MLA_QW].astype(BF16)


def _ukv_proj(proj, w_ukv, kpe_pad, mp):
    tm = _pick(mp, ROW_TILES[:-1])
    n = MLA_HEADS * MLA_QW
    return pl.pallas_call(
        _ukv_kernel,
        grid=(mp // tm,),
        in_specs=[
            pl.BlockSpec((tm, MLA_RANK), lambda i: (i, EV_CKV_BLOCK)),
            pl.BlockSpec((MLA_RANK, n), lambda i: (0, 0)),
            pl.BlockSpec((tm, LANE), lambda i: (i, 0)),
        ],
        out_specs=[pl.BlockSpec((tm, n), lambda i: (i, 0)),
                   pl.BlockSpec((tm, MLA_HEADS * MLA_V), lambda i: (i, 0))],
        out_shape=[jax.ShapeDtypeStruct((mp, n), BF16),
                   jax.ShapeDtypeStruct((mp, MLA_HEADS * MLA_V), BF16)],
        compiler_params=_cp("parallel"),
        name="mla_kv_up_proj",
    )(proj, w_ukv, kpe_pad)


def _ret_core(q, k, v, g, s0, dec, qd, kd, cd, gn_g, gn_b):
    qb, kb, vb = q.astype(BF16), k.astype(BF16), v.astype(BF16)
    s = _dot_nt(qb, kb) * dec
    o = _dot(s.astype(BF16), vb)
    o = o + _dot(qb, s0.astype(BF16)) * qd
    kk_t = (k * kd).T.astype(BF16)
    s_new = s0 * cd + _dot(kk_t, vb)
    on = _layer_norm(o, gn_g, gn_b)
    return on * (g * jax.nn.sigmoid(g)), s_new


def _ret_prompt_kernel(q_ref, k_ref, v_ref, g_ref, dec_ref, qd_ref, kd_ref, cd_ref, gg_ref, gb_ref,
                       o_ref, s_out_ref, s_ref, *, nc):
    c = pl.program_id(2)

    @pl.when(c == 0)
    def _():
        s_ref[...] = jnp.zeros_like(s_ref)

    for h in range(RET_HEAD_GROUP):
        cols = slice(h * RET_D, (h + 1) * RET_D)
        out, s_new = _ret_core(q_ref[:, cols], k_ref[:, cols], v_ref[:, cols], g_ref[:, cols], s_ref[h],
                               dec_ref[h], qd_ref[h], kd_ref[h], cd_ref[h], gg_ref[h], gb_ref[h])
        o_ref[:, cols] = out
        s_ref[h] = s_new

    @pl.when(c == nc - 1)
    def _():
        s_out_ref[...] = s_ref[...]


RET_HEAD_GROUP = 4


def _ret_prompt(proj, tabs, gn_g, gn_b, batch, seq):
    nc = seq // RET_CHUNK
    hg = RET_HEAD_GROUP
    gw = hg * RET_D
    groups = RET_HEADS // hg

    def tok(section):
        return pl.BlockSpec((RET_CHUNK, gw), lambda b, h, c: (b * nc + c, section * groups + h))

    def per_head(shape):
        return pl.BlockSpec((hg,) + shape, lambda b, h, c: (h, 0, 0))

    dec, qd, kd, cd = tabs
    return pl.pallas_call(
        functools.partial(_ret_prompt_kernel, nc=nc),
        grid=(batch, groups, nc),
        in_specs=[tok(0), tok(1), tok(2), tok(3),
                  per_head((RET_CHUNK, RET_CHUNK)), per_head((RET_CHUNK, LANE)), per_head((RET_CHUNK, LANE)),
                  per_head((1, LANE)), per_head((1, RET_D)), per_head((1, RET_D))],
        out_specs=[pl.BlockSpec((RET_CHUNK, gw), lambda b, h, c: (b * nc + c, h)),
                   pl.BlockSpec((None, hg, RET_D, RET_D), lambda b, h, c: (b, h, 0, 0))],
        out_shape=[jax.ShapeDtypeStruct((batch * seq, RET_HEADS * RET_D), F32),
                   jax.ShapeDtypeStruct((batch, RET_HEADS, RET_D, RET_D), F32)],
        scratch_shapes=[pltpu.VMEM((hg, RET_D, RET_D), F32)],
        compiler_params=_cp("parallel", "parallel", "arbitrary"),
        name="retention_prompt",
    )(proj, proj, proj, proj, dec, qd, kd, cd, gn_g[:, None, :], gn_b[:, None, :])


def _ret_sample_kernel(q_ref, k_ref, v_ref, g_ref, s0_ref, dec_ref, qd_ref, kd_ref, cd_ref, gg_ref, gb_ref,
                       o_ref, s_out_ref, *, ts):
    pad = jnp.zeros((RET_CHUNK - ts, RET_D), F32)

    def padded(ref, h):
        return jnp.concatenate([ref[:, h * RET_D:(h + 1) * RET_D], pad], axis=0)

    for h in range(RET_HEADS):
        out, s_new = _ret_core(padded(q_ref, h), padded(k_ref, h), padded(v_ref, h), padded(g_ref, h),
                               s0_ref[h], dec_ref[h], qd_ref[h], kd_ref[h], cd_ref[h], gg_ref[h], gb_ref[h])
        o_ref[:, h * RET_D:(h + 1) * RET_D] = out[:ts]
        s_out_ref[h] = s_new


def _ret_sample(proj, state, tabs, gn_g, gn_b, mp, bd, ts):
    width = RET_HEADS * RET_D
    row0 = mp // ts

    def tok(j):
        return pl.BlockSpec((ts, width), lambda b: (row0 + b, j))

    def const(a):
        return pl.BlockSpec(a.shape, lambda b: (0,) * a.ndim)

    dec, qd, kd, cd = tabs
    gg, gb = gn_g[:, None, :], gn_b[:, None, :]
    return pl.pallas_call(
        functools.partial(_ret_sample_kernel, ts=ts),
        grid=(bd,),
        in_specs=[tok(0), tok(1), tok(2), tok(3),
                  pl.BlockSpec((None, RET_HEADS, RET_D, RET_D), lambda b: (b, 0, 0, 0)),
                  const(dec), const(qd), const(kd), const(cd), const(gg), const(gb)],
        out_specs=[pl.BlockSpec((ts, width), lambda b: (b, 0)),
                   pl.BlockSpec((None, RET_HEADS, RET_D, RET_D), lambda b: (b, 0, 0, 0))],
        out_shape=[jax.ShapeDtypeStruct((bd * ts, width), F32),
                   jax.ShapeDtypeStruct(state.shape, F32)],
        compiler_params=_cp("parallel"),
        name="retention_sample",
    )(proj, proj, proj, proj, state, dec, qd, kd, cd, gg, gb)


def _ret_tables(chunk_len):
    h = jnp.arange(RET_HEADS, dtype=F32)
    lg = jnp.log(1.0 - jnp.exp2(-5.0 - h))
    idx = jnp.arange(RET_CHUNK, dtype=F32)
    diff = idx[:, None] - idx[None, :]
    dec = jnp.where(diff >= 0, jnp.exp(jnp.maximum(diff, 0.0)[None] * lg[:, None, None]), 0.0)
    qd = jnp.exp((idx + 1.0)[None, :] * lg[:, None])
    kd = jnp.exp((chunk_len - 1.0 - idx)[None, :] * lg[:, None])
    kd = jnp.where(idx[None, :] < chunk_len, kd, 0.0)
    cd = jnp.exp(chunk_len * lg)
    rep = lambda a: jnp.broadcast_to(a[..., None], a.shape + (LANE,))
    return dec, rep(qd), rep(kd), rep(cd[:, None])


def _mla_prompt_kernel(q_ref, k_ref, v_ref, o_ref, *, tq):
    i = pl.program_id(2)
    q = q_ref[...]

    def step(j, carry, masked):
        m, l, acc = carry
        rows = pl.ds(pl.multiple_of(j * tq, tq), tq)
        s = _dot_nt(q, k_ref[rows, :]) * MLA_SCALE
        if masked:
            r = lax.broadcasted_iota(jnp.int32, (tq, tq), 0)
            c = lax.broadcasted_iota(jnp.int32, (tq, tq), 1)
            s = jnp.where(c <= r, s, NEG)
        m_new = jnp.maximum(m, jnp.max(s, axis=1, keepdims=True))
        a = jnp.exp(m - m_new)
        p = jnp.exp(s - m_new)
        l = a * l + jnp.sum(p, axis=1, keepdims=True)
        acc = a * acc + _dot(p.astype(BF16), v_ref[rows, :])
        return m_new, l, acc

    init = (jnp.full((tq, 1), NEG, F32), jnp.zeros((tq, 1), F32), jnp.zeros((tq, MLA_V), F32))
    carry = lax.fori_loop(0, i, lambda j, c: step(j, c, False), init)
    m, l, acc = step(i, carry, True)
    o_ref[...] = acc / l


def _mla_prompt(q_cat, k_cat, v, batch, seq):
    tq = _pick(seq, (1024, 512, 256, 128))
    nq = seq // tq
    return pl.pallas_call(
        functools.partial(_mla_prompt_kernel, tq=tq),
        grid=(batch, MLA_HEADS, nq),
        in_specs=[pl.BlockSpec((tq, MLA_QW), lambda b, h, i: (b * nq + i, h)),
                  pl.BlockSpec((seq, MLA_QW), lambda b, h, i: (b, h)),
                  pl.BlockSpec((seq, MLA_V), lambda b, h, i: (b, h))],
        out_specs=pl.BlockSpec((tq, MLA_V), lambda b, h, i: (b * nq + i, h)),
        out_shape=jax.ShapeDtypeStruct((batch * seq, MLA_HEADS * MLA_V), F32),
        compiler_params=_cp("parallel", "parallel", "arbitrary"),
        name="mla_prompt_attention",
    )(q_cat, k_cat, v)


def _qlat_kernel(q_ref, w_ref, o_ref):
    o_ref[...] = _dot(q_ref[...], w_ref[...])


def _q_latent(q_cat, w_uk_t, mp, ms):
    tm = _pick(ms, ROW_TILES[:-1])
    row0 = mp // tm
    return pl.pallas_call(
        _qlat_kernel,
        grid=(MLA_HEADS, ms // tm),
        in_specs=[pl.BlockSpec((tm, MLA_NOPE), lambda h, i: (row0 + i, 2 * h)),
                  pl.BlockSpec((None, MLA_NOPE, MLA_RANK), lambda h, i: (h, 0, 0))],
        out_specs=pl.BlockSpec((None, tm, MLA_RANK), lambda h, i: (h, i, 0)),
        out_shape=jax.ShapeDtypeStruct((MLA_HEADS, ms, MLA_RANK), F32),
        compiler_params=_cp("parallel", "parallel"),
        name="mla_q_latent",
    )(q_cat, w_uk_t)


def _mla_sample_kernel(pt_ref, ql_ref, qp_ref, ckvn_ref, kpen_ref, ckv_hbm, kpe_hbm, o_ref,
                       ckv_buf, kpe_buf, kv_ref, kp_ref, sem, *, layer, npg, nchunk, ts):
    b = pl.program_id(0)
    nb = pl.num_programs(0)
    last = nb * nchunk - 1
    rows = MLA_HEADS * ts

    def chunk_copies(g, slot):
        cps = []
        for j in range(npg):
            page = pt_ref[g * npg + j]
            cps.append(pltpu.make_async_copy(ckv_hbm.at[layer, page], ckv_buf.at[slot, pl.ds(j * PAGE, PAGE)],
                                             sem.at[0, slot]))
            cps.append(pltpu.make_async_copy(kpe_hbm.at[layer, page], kpe_buf.at[slot, j], sem.at[1, slot]))
        return cps

    def ahead(g, k):
        return chunk_copies(jnp.minimum(g + k, last), lax.rem(g + k, MLA_SLOTS))

    @pl.when(b == 0)
    def _():
        for k in range(MLA_SLOTS - 1):
            for cp in ahead(0, k):
                cp.start()

    ql = ql_ref[...].reshape(rows, MLA_RANK).astype(BF16)
    qp = qp_ref[...].astype(BF16)

    def update(state, s, v):
        m, l, acc = state
        m_new = jnp.maximum(m, jnp.max(s, axis=1, keepdims=True))
        a = jnp.exp(m - m_new)
        p = jnp.exp(s - m_new)
        return m_new, a * l + jnp.sum(p, axis=1, keepdims=True), a * acc + _dot(p.astype(BF16), v)

    state = (jnp.full((rows, 1), NEG, F32), jnp.zeros((rows, 1), F32), jnp.zeros((rows, MLA_RANK), F32))
    for c in range(nchunk):
        g = b * nchunk + c
        for cp in ahead(g, MLA_SLOTS - 1):
            cp.start()
        slot = lax.rem(g, MLA_SLOTS)
        for cp in chunk_copies(g, slot):
            cp.wait()
        kvb, kpb = kv_ref.at[c % 2], kp_ref.at[c % 2]
        kvb[...] = ckv_buf[slot].astype(BF16)
        for j in range(npg):
            kpb[:, j * PAGE:(j + 1) * PAGE] = kpe_buf[slot, j].astype(BF16)
        kv = kvb[...]
        state = update(state, (_dot_nt(ql, kv) + _dot(qp, kpb[...])) * MLA_SCALE, kv)

    ckv = jnp.concatenate([ckvn_ref[...], jnp.zeros((PAGE - ts, MLA_RANK), F32)], axis=0).astype(BF16)
    kpe = jnp.concatenate([kpen_ref[...], jnp.zeros((PAGE - ts, MLA_ROPE), F32)], axis=0).astype(BF16)
    s = (_dot_nt(ql, ckv) + _dot_nt(qp, kpe)) * MLA_SCALE
    t = lax.broadcasted_iota(jnp.int32, (rows, PAGE), 0) % ts
    key = lax.broadcasted_iota(jnp.int32, (rows, PAGE), 1)
    _, l, acc = update(state, jnp.where(key <= t, s, NEG), ckv)
    o_ref[...] = acc / l

    @pl.when(b == nb - 1)
    def _():
        for k in range(1, MLA_SLOTS):
            for cp in ahead(last, k):
                cp.wait()


MLA_CHUNK_PAGES = (16, 8, 4, 2, 1)
MLA_SLOTS = 3


def _mla_sample(page_table_flat, q_lat, q_pe, ckv_new, kpe_new, cache_ckv, cache_kpe_t, layer, bd, ts, n_pages):
    npg = _pick(n_pages, MLA_CHUNK_PAGES)
    nchunk = n_pages // npg
    rows = MLA_HEADS * ts
    in_specs = [pl.BlockSpec((MLA_HEADS, ts, MLA_RANK), lambda b, pt: (0, b, 0)),
                pl.BlockSpec((None, rows, MLA_ROPE), lambda b, pt: (b, 0, 0)),
                pl.BlockSpec((None, ts, MLA_RANK), lambda b, pt: (b, 0, 0)),
                pl.BlockSpec((None, ts, MLA_ROPE), lambda b, pt: (b, 0, 0)),
                pl.BlockSpec(memory_space=pl.ANY),
                pl.BlockSpec(memory_space=pl.ANY)]
    return pl.pallas_call(
        functools.partial(_mla_sample_kernel, layer=layer, npg=npg, nchunk=nchunk, ts=ts),
        grid_spec=pltpu.PrefetchScalarGridSpec(
            num_scalar_prefetch=1,
            grid=(bd,),
            in_specs=in_specs,
            out_specs=pl.BlockSpec((None, rows, MLA_RANK), lambda b, pt: (b, 0, 0)),
            scratch_shapes=[pltpu.VMEM((MLA_SLOTS, npg * PAGE, MLA_RANK), F32),
                            pltpu.VMEM((MLA_SLOTS, npg, MLA_ROPE, PAGE), F32),
                            pltpu.VMEM((2, npg * PAGE, MLA_RANK), BF16), pltpu.VMEM((2, MLA_ROPE, npg * PAGE), BF16),
                            pltpu.SemaphoreType.DMA((2, MLA_SLOTS))],
        ),
        out_shape=jax.ShapeDtypeStruct((bd, rows, MLA_RANK), F32),
        compiler_params=_cp("arbitrary"),
        name="mla_sample_attention",
    )(page_table_flat, q_lat, q_pe, ckv_new, kpe_new, cache_ckv, cache_kpe_t)


def _uv_kernel(o_ref, w_ref, out_ref, *, ts):
    tb = o_ref.shape[0]
    o = o_ref[...].reshape(tb * ts, MLA_RANK).astype(BF16)
    out_ref[...] = _dot(o, w_ref[...])


def _mla_value_up(o_lat, w_uv, bd, ts):
    tb = _pick(bd, (64, 32, 16, 8, 4, 2, 1))
    return pl.pallas_call(
        functools.partial(_uv_kernel, ts=ts),
        grid=(MLA_HEADS, bd // tb),
        in_specs=[pl.BlockSpec((tb, None, ts, MLA_RANK), lambda h, i: (i, h, 0, 0)),
                  pl.BlockSpec((None, MLA_RANK, MLA_V), lambda h, i: (h, 0, 0))],
        out_specs=pl.BlockSpec((tb * ts, MLA_V), lambda h, i: (i, h)),
        out_shape=jax.ShapeDtypeStruct((bd * ts, MLA_HEADS * MLA_V), F32),
        compiler_params=_cp("parallel", "parallel"),
        name="mla_value_up_proj",
    )(o_lat, w_uv)


def _mm_kernel(x_ref, w_ref, o_ref, xb_ref):
    @pl.when(pl.program_id(1) == 0)
    def _():
        xb_ref[...] = x_ref[...].astype(BF16)

    o_ref[...] = _dot(xb_ref[...], w_ref[...])


def _matmul(x, w):
    m, d = x.shape
    n = w.shape[1]
    tm = _pick(m, PROJ_ROW_TILES)
    tn = _pick(n, (512, 256, 128))
    return pl.pallas_call(
        _mm_kernel,
        grid=(m // tm, n // tn),
        in_specs=[pl.BlockSpec((tm, d), lambda i, j: (i, 0)),
                  pl.BlockSpec((d, tn), lambda i, j: (0, j))],
        out_specs=pl.BlockSpec((tm, tn), lambda i, j: (i, j)),
        out_shape=jax.ShapeDtypeStruct((m, n), F32),
        scratch_shapes=[pltpu.VMEM((tm, d), BF16)],
        compiler_params=_cp("parallel", "arbitrary"),
        name="sb_in_proj",
    )(x, w)


SB_Q_COLS = SB_HEADS * SB_DIM
SB_KV_COLS = SB_KV_HEADS * SB_DIM
SB_TK = 128
SB_EXIT = -104.0


def _sb_block(z, mask, carry, tri_ones):
    r = z.shape[0]
    sp = jnp.maximum(z, 0.0) + jnp.log1p(jnp.exp(-jnp.abs(z)))
    if mask is not None:
        sp_m = jnp.where(mask, sp, 0.0)
    else:
        sp_m = sp
    hi = sp_m.astype(BF16)
    lo = (sp_m - hi.astype(F32)).astype(BF16)
    cs = _dot(jnp.concatenate([hi, lo], axis=0), tri_ones)
    cs = cs[:r] + cs[r:]
    a = jnp.exp(z - sp + carry - cs[:, :SB_TK])
    if mask is not None:
        a = jnp.where(mask, a, 0.0)
    return a, carry - cs[:, SB_TK:]


def _sb_live(carry):
    return (jnp.max(carry) > SB_EXIT).astype(jnp.int32)


def _sb_prompt_kernel(q_ref, k_ref, v_ref, to_ref, o_ref, *, tq):
    i = pl.program_id(2)
    rows = SB_GROUP * tq
    q = jnp.concatenate([q_ref[:, g * SB_DIM:(g + 1) * SB_DIM] for g in range(SB_GROUP)], axis=0).astype(BF16)
    tri_ones = to_ref[...]

    def step(j, carry, acc, mask):
        ks = pl.ds(pl.multiple_of(j * SB_TK, SB_TK), SB_TK)
        z = _dot_nt(q, k_ref[ks, :].astype(BF16)) * SB_SCALE
        a, carry = _sb_block(z, mask, carry, tri_ones)
        return carry, acc + _dot(a.astype(BF16), v_ref[ks, :].astype(BF16))

    t = lax.broadcasted_iota(jnp.int32, (rows, SB_TK), 0) % tq
    key = lax.broadcasted_iota(jnp.int32, (rows, SB_TK), 1)
    carry, acc = step(i, jnp.zeros((rows, SB_TK), F32), jnp.zeros((rows, SB_DIM), F32), key < t)

    def body(state):
        j, _, carry, acc = state
        carry, acc = step(j, carry, acc, None)
        return j - 1, _sb_live(carry), carry, acc

    _, _, carry, acc = lax.while_loop(lambda s: (s[0] >= 0) & (s[1] > 0), body,
                                      (i - 1, _sb_live(carry), carry, acc))
    for g in range(SB_GROUP):
        o_ref[:, g * SB_DIM:(g + 1) * SB_DIM] = acc[g * tq:(g + 1) * tq]


def _sb_prompt(proj, tri_ones, batch, seq):
    tq = SB_TK
    nq = seq // tq
    gw = SB_GROUP * SB_DIM
    kb0 = SB_Q_COLS // SB_DIM
    vb0 = kb0 + SB_KV_HEADS
    return pl.pallas_call(
        functools.partial(_sb_prompt_kernel, tq=tq),
        grid=(batch, SB_KV_HEADS, nq),
        in_specs=[pl.BlockSpec((tq, gw), lambda b, h, i: (b * nq + i, h)),
                  pl.BlockSpec((seq, SB_DIM), lambda b, h, i: (b, kb0 + h)),
                  pl.BlockSpec((seq, SB_DIM), lambda b, h, i: (b, vb0 + h)),
                  pl.BlockSpec((SB_TK, 2 * SB_TK), lambda b, h, i: (0, 0))],
        out_specs=pl.BlockSpec((tq, gw), lambda b, h, i: (b * nq + i, h)),
        out_shape=jax.ShapeDtypeStruct((batch * seq, SB_Q_COLS), F32),
        compiler_params=_cp("parallel", "parallel", "arbitrary"),
        name="sb_prompt_attention",
    )(proj, proj, proj, tri_ones)


def _sb_sample_kernel(pt_ref, q_ref, kn_ref, vn_ref, to_ref, kc_ref, vc_ref, o_ref,
                      kfirst, vfirst, kmore, vmore, sem_first, sem_more, *, ts, npg, nchunk):
    b = pl.program_id(0)
    nb = pl.num_programs(0)
    n_pages = npg * nchunk
    page_rows = PAGE * SB_KV_HEADS
    hrows = SB_GROUP * ts
    rows = SB_KV_HEADS * hrows
    q = jnp.concatenate([q_ref[:, h * SB_DIM:(h + 1) * SB_DIM] for h in range(SB_HEADS)], axis=0).astype(BF16)
    tri_ones = to_ref[...]

    def chunk_copies(seq, c, kbuf, vbuf, ksem, vsem):
        cps = []
        for j in range(npg):
            page = pt_ref[seq * n_pages + n_pages - 1 - (c * npg + j)]
            rows_j = pl.ds(j * page_rows, page_rows)
            cps.append(pltpu.make_async_copy(kc_ref.at[page], kbuf.at[rows_j], ksem))
            cps.append(pltpu.make_async_copy(vc_ref.at[page], vbuf.at[rows_j], vsem))
        return cps

    def first_copies(seq, slot):
        return chunk_copies(seq, 0, kfirst.at[slot], vfirst.at[slot], sem_first.at[0, slot], sem_first.at[1, slot])

    slot = lax.rem(b, 2)

    @pl.when(b == 0)
    def _():
        for cp in first_copies(b, slot):
            cp.start()

    nxt = first_copies(jnp.minimum(b + 1, nb - 1), 1 - slot)
    for cp in nxt:
        cp.start()

    def block(k_of, v_of, mask, carry, acc):
        z = jnp.concatenate(
            [_dot_nt(q[h * hrows:(h + 1) * hrows], k_of(h).astype(BF16)) for h in range(SB_KV_HEADS)],
            axis=0) * SB_SCALE
        a, carry = _sb_block(z, mask, carry, tri_ones)
        ab = a.astype(BF16)
        upd = jnp.concatenate(
            [_dot(ab[h * hrows:(h + 1) * hrows], v_of(h).astype(BF16)) for h in range(SB_KV_HEADS)],
            axis=0)
        return carry, acc + upd

    zpad = jnp.zeros((PAGE - ts, SB_KV_COLS), F32)
    kn = jnp.concatenate([kn_ref[...], zpad], axis=0)
    vn = jnp.concatenate([vn_ref[...], zpad], axis=0)
    t = lax.broadcasted_iota(jnp.int32, (rows, SB_TK), 0) % ts
    key = lax.broadcasted_iota(jnp.int32, (rows, SB_TK), 1)
    carry, acc = block(lambda h: kn[:, h * SB_DIM:(h + 1) * SB_DIM], lambda h: vn[:, h * SB_DIM:(h + 1) * SB_DIM],
                       key < t, jnp.zeros((rows, SB_TK), F32), jnp.zeros((rows, SB_DIM), F32))

    def chunk_blocks(k_rows, v_rows, carry, acc):
        for j in range(npg):
            def head(read, h, j=j):
                return read(pl.ds(j * page_rows + h, PAGE, stride=SB_KV_HEADS))
            carry, acc = block(lambda h: head(k_rows, h), lambda h: head(v_rows, h), None, carry, acc)
        return carry, acc

    for cp in first_copies(b, slot):
        cp.wait()
    carry, acc = chunk_blocks(lambda r: kfirst[slot, r, :], lambda r: vfirst[slot, r, :], carry, acc)

    def body(state):
        c, _, carry, acc = state
        cps = chunk_copies(b, c, kmore, vmore, sem_more.at[0], sem_more.at[1])
        for cp in cps:
            cp.start()
        for cp in cps:
            cp.wait()
        carry, acc = chunk_blocks(lambda r: kmore[r, :], lambda r: vmore[r, :], carry, acc)
        return c + 1, _sb_live(carry), carry, acc

    _, _, carry, acc = lax.while_loop(lambda s: (s[0] < nchunk) & (s[1] > 0), body,
                                      (jnp.int32(1), _sb_live(carry), carry, acc))

    for h in range(SB_HEADS):
        o_ref[:, h * SB_DIM:(h + 1) * SB_DIM] = acc[h * ts:(h + 1) * ts]

    @pl.when(b == nb - 1)
    def _():
        for cp in nxt:
            cp.wait()


SB_CHUNK_PAGES = (2, 1)


def _sb_sample(page_table_flat, proj, tri_ones, cache_k, cache_v, mp, bd, ts, n_pages):
    row0 = mp // ts
    kb0 = SB_Q_COLS // SB_KV_COLS
    npg = _pick(n_pages, SB_CHUNK_PAGES)
    chunk_rows = npg * PAGE * SB_KV_HEADS
    in_specs = [pl.BlockSpec((ts, SB_Q_COLS), lambda b, pt: (row0 + b, 0)),
                pl.BlockSpec((ts, SB_KV_COLS), lambda b, pt: (row0 + b, kb0)),
                pl.BlockSpec((ts, SB_KV_COLS), lambda b, pt: (row0 + b, kb0 + 1)),
                pl.BlockSpec((SB_TK, 2 * SB_TK), lambda b, pt: (0, 0)),
                pl.BlockSpec(memory_space=pl.ANY),
                pl.BlockSpec(memory_space=pl.ANY)]
    return pl.pallas_call(
        functools.partial(_sb_sample_kernel, ts=ts, npg=npg, nchunk=n_pages // npg),
        grid_spec=pltpu.PrefetchScalarGridSpec(
            num_scalar_prefetch=1,
            grid=(bd,),
            in_specs=in_specs,
            out_specs=pl.BlockSpec((ts, SB_Q_COLS), lambda b, pt: (b, 0)),
            scratch_shapes=[pltpu.VMEM((2, chunk_rows, SB_DIM), F32), pltpu.VMEM((2, chunk_rows, SB_DIM), F32),
                            pltpu.VMEM((chunk_rows, SB_DIM), F32), pltpu.VMEM((chunk_rows, SB_DIM), F32),
                            pltpu.SemaphoreType.DMA((2, 2)), pltpu.SemaphoreType.DMA((2,))],
        ),
        out_shape=jax.ShapeDtypeStruct((bd * ts, SB_Q_COLS), F32),
        compiler_params=_cp("arbitrary"),
        name="sb_sample_attention",
    )(page_table_flat, proj, proj, proj, tri_ones, cache_k, cache_v)


def _rope_tables(pos):
    p = pos.astype(F32)[:, None]
    inv_r = 1.0 / (ROPE_BASE ** jnp.linspace(0.0, 1.0, RET_D // 2, dtype=F32))
    ang = p * inv_r[None, :]
    c, s = jnp.cos(ang), jnp.sin(ang)
    cr = jnp.concatenate([c, c], axis=1)
    sr = jnp.concatenate([-s, s], axis=1)
    inv_m = 1.0 / (ROPE_BASE ** (jnp.arange(0, MLA_ROPE, 2, dtype=F32) / MLA_ROPE))
    ang = p * inv_m[None, :]
    c, s = jnp.cos(ang), jnp.sin(ang)
    z = jnp.zeros((pos.shape[0], LANE - MLA_ROPE), F32)
    t1 = jnp.concatenate([c, c, z], axis=1)
    t2 = jnp.concatenate([-s, s, z], axis=1)
    return cr, sr, t1, t2


def _swap_halves(w):
    half = w.shape[-1] // 2
    return jnp.concatenate([w[..., half:], w[..., :half]], axis=-1)


def kernel(x_prompt, x_sample, state_ret, cache_mla_ckv, cache_mla_kpe, cache_sb_k, cache_sb_v, page_table,
           ln_g, ln_b, ffn_w_gate, ffn_w_up, ffn_w_down,
           ev_w_in, ev_q_norm, ev_w_uq, ev_kv_norm, ev_w_ukv, ev_gn_g, ev_gn_b, ev_w_o,
           od_w_in, od_w_o):
    batch, seq, d = x_prompt.shape
    bd, ts, _ = x_sample.shape
    n_pages = page_table.shape[1]
    past = n_pages * PAGE
    mp, ms = batch * seq, bd * ts
    assert seq % RET_CHUNK == 0 and ts <= 8 and mp % ts == 0

    x = jnp.concatenate([x_prompt.reshape(mp, d), x_sample.reshape(ms, d)], axis=0)
    pos = jnp.concatenate([jnp.tile(jnp.arange(seq), batch), jnp.tile(past + jnp.arange(ts), bd)])
    cr, sr, t1, t2 = _rope_tables(pos)
    pt_flat = page_table.reshape(-1)
    tri = (jnp.arange(SB_TK)[:, None] > jnp.arange(SB_TK)[None, :]).astype(BF16)
    tri_ones = jnp.concatenate([tri, jnp.ones((SB_TK, SB_TK), BF16)], axis=1)
    tabs_p = _ret_tables(float(RET_CHUNK))
    tabs_s = _ret_tables(float(ts))

    wg, wu, wd = ffn_w_gate.astype(BF16), ffn_w_up.astype(BF16), ffn_w_down.astype(BF16)

    outs = {k: [] for k in ("ret_p", "ret_s", "ckv_p", "ckv_s", "kpe_p", "kpe_s", "sbk_p", "sbk_s", "sbv_p", "sbv_s")}
    for l in range(DEPTH):
        x = _ffn_half_step(x, wg, wu, wd, l, 0, ln_g[l, 0], ln_b[l, 0])
        if l % 2 == 0:
            e = l // 2
            w_in = ev_w_in[e]
            w_main = w_in[:, :EV_MAIN_COLS].astype(BF16)
            w_kpe = w_in[:, EV_MAIN_COLS:]
            w_kpe = jnp.concatenate([w_kpe, _swap_halves(w_kpe)], axis=1).astype(BF16)
            proj = _ev_proj(x, w_main, cr, sr, ev_q_norm[e], ev_kv_norm[e])
            kpe_pad = _kpe_proj(x, w_kpe, t1, t2)

            w_uq = ev_w_uq[e].reshape(MLA_RANK, MLA_HEADS, MLA_NOPE + MLA_ROPE)
            w_uq_ext = jnp.concatenate([w_uq, _swap_halves(w_uq[..., MLA_NOPE:])], axis=-1)
            q_cat = _uq_proj(proj, w_uq_ext.reshape(MLA_RANK, MLA_HEADS * MLA_QW).astype(BF16), t1, t2)

            w_ukv = ev_w_ukv[e]
            k_cat, v_p = _ukv_proj(proj, w_ukv.astype(BF16), kpe_pad, mp)

            ret_out_p, s_p = _ret_prompt(proj, tabs_p, ev_gn_g[e], ev_gn_b[e], batch, seq)
            ret_out_s, s_s = _ret_sample(proj, state_ret[e], tabs_s, ev_gn_g[e], ev_gn_b[e], mp, bd, ts)
            mla_out_p = _mla_prompt(q_cat, k_cat, v_p, batch, seq)

            w3 = w_ukv.reshape(MLA_RANK, MLA_HEADS, MLA_NOPE + MLA_V)
            w_uk_t = w3[..., :MLA_NOPE].transpose(1, 2, 0).astype(BF16)
            w_uv = w3[..., MLA_NOPE:].transpose(1, 0, 2).astype(BF16)
            q_lat = _q_latent(q_cat, w_uk_t, mp, ms)
            q_pe_s = q_cat[mp:].reshape(bd, ts, MLA_HEADS, MLA_QW)[..., MLA_NOPE:MLA_NOPE + MLA_ROPE]
            q_pe_s = q_pe_s.transpose(0, 2, 1, 3).reshape(bd, MLA_HEADS * ts, MLA_ROPE)
            ckv = proj[:, EV_CKV_BLOCK * EV_BLOCK:(EV_CKV_BLOCK + 1) * EV_BLOCK]
            kpe = kpe_pad[:, :MLA_ROPE]
            ckv_s = ckv[mp:].reshape(bd, ts, MLA_RANK)
            kpe_s = kpe[mp:].reshape(bd, ts, MLA_ROPE)
            o_lat = _mla_sample(pt_flat, q_lat, q_pe_s, ckv_s, kpe_s, cache_mla_ckv,
                                jnp.swapaxes(cache_mla_kpe, 2, 3), e, bd, ts, n_pages)
            mla_out_s = _mla_value_up(o_lat.reshape(bd, MLA_HEADS, ts, MLA_RANK), w_uv, bd, ts)

            w_o = ev_w_o[e].astype(BF16)
            n_ret = RET_HEADS * RET_D
            x = _out_proj_ln([(ret_out_p, ret_out_s), (mla_out_p, mla_out_s)],
                             [w_o[:n_ret], w_o[n_ret:]], x, ln_g[l, 1], ln_b[l, 1])
            outs["ret_p"].append(s_p)
            outs["ret_s"].append(s_s)
            outs["ckv_p"].append(ckv[:mp].reshape(batch, seq, MLA_RANK))
            outs["ckv_s"].append(ckv_s)
            outs["kpe_p"].append(kpe[:mp].reshape(batch, seq, MLA_ROPE))
            outs["kpe_s"].append(kpe_s)
        else:
            o = l // 2
            proj = _matmul(x, od_w_in[o].astype(BF16))
            n_kv = cache_sb_k.shape[1]
            cache_k = cache_sb_k[o].reshape(n_kv, PAGE * SB_KV_HEADS, SB_DIM)
            cache_v = cache_sb_v[o].reshape(n_kv, PAGE * SB_KV_HEADS, SB_DIM)
            att_p = _sb_prompt(proj, tri_ones, batch, seq)
            att_s = _sb_sample(pt_flat, proj, tri_ones, cache_k, cache_v, mp, bd, ts, n_pages)
            x = _out_proj_ln([(att_p, att_s)], [od_w_o[o].astype(BF16)], x, ln_g[l, 1], ln_b[l, 1])
            k = proj[:, SB_Q_COLS:SB_Q_COLS + SB_KV_COLS]
            v = proj[:, SB_Q_COLS + SB_KV_COLS:]
            outs["sbk_p"].append(k[:mp].reshape(batch, seq, SB_KV_HEADS, SB_DIM))
            outs["sbk_s"].append(k[mp:].reshape(bd, ts, SB_KV_HEADS, SB_DIM))
            outs["sbv_p"].append(v[:mp].reshape(batch, seq, SB_KV_HEADS, SB_DIM))
            outs["sbv_s"].append(v[mp:].reshape(bd, ts, SB_KV_HEADS, SB_DIM))
        x = _ffn_half_step(x, wg, wu, wd, l, 1, ln_g[l, 2], ln_b[l, 2])

    y_p = x[:mp].reshape(batch, seq, d)
    y_s = x[mp:].reshape(bd, ts, d)
    return (y_p, y_s, jnp.stack(outs["ret_p"]), jnp.stack(outs["ret_s"]),
            jnp.stack(outs["ckv_p"]), jnp.stack(outs["ckv_s"]), jnp.stack(outs["kpe_p"]), jnp.stack(outs["kpe_s"]),
            jnp.stack(outs["sbk_p"]), jnp.stack(outs["sbk_s"]), jnp.stack(outs["sbv_p"]), jnp.stack(outs["sbv_s"]))
```
